```python
import math
import jax, jax.numpy as jnp
from jax import lax
import numpy as np

D_MODEL = 1024
BATCH = 16
SEQ = 2048
DEPTH = 4

GRID_W = 64
CTX_LEN = 256
N_MIXERS = 3
N_LAYERS_A = (DEPTH + 2) // 3
N_LAYERS_B = (DEPTH + 1) // 3
N_LAYERS_C = DEPTH // 3
EPS = 1e-6

DA_HEADS = 8
DA_QK_DIM = 64
DA_V_DIM = 2 * DA_QK_DIM
Q_BLOCK = 128
ROPE_BASE = 10000.0

LRU_WIDTH = 1280
LRU_BLOCKS = 10
LRU_BLOCK_DIM = LRU_WIDTH // LRU_BLOCKS
CONV_W = 4
LRU_C = 8.0

POOL_GROUPS = 4
POOL_WINDOWS = (2, 4, 8, 16)
POOL_GROUP_DIM = D_MODEL // POOL_GROUPS

N_GROUPS = 4
EXPERTS_PER_GROUP = 4
N_EXPERTS = N_GROUPS * EXPERTS_PER_GROUP
TOP_K = 2
D_EXPERT = 512

kernel_name = "hybrid_diffattn_rglru_pool_hmoe_prefix_dit"


def rms_norm(x, g):
    xf = x.astype(jnp.float32)
    y = xf * lax.rsqrt(jnp.mean(xf * xf, axis=-1, keepdims=True) + EPS)
    return (y * g.astype(jnp.float32)).astype(x.dtype)


def modulate(h, shift, scale):
    return h * (1 + scale) + shift


def axial_rope_tables(n_tokens):
    rows = n_tokens // GRID_W
    row = jnp.repeat(jnp.arange(rows), GRID_W).astype(jnp.float32)
    col = jnp.tile(jnp.arange(GRID_W), rows).astype(jnp.float32)
    n_freq = DA_QK_DIM // 4
    inv = ROPE_BASE ** (-jnp.arange(n_freq, dtype=jnp.float32) / n_freq)
    ang = jnp.concatenate([row[:, None] * inv, col[:, None] * inv], axis=-1)
    return jnp.cos(ang), jnp.sin(ang)


def apply_rope(x, cos, sin):
    x1, x2 = jnp.split(x, 2, axis=-1)
    cos = cos[None, :, None, None, :].astype(x.dtype)
    sin = sin[None, :, None, None, :].astype(x.dtype)
    return jnp.concatenate([x1 * cos - x2 * sin, x1 * sin + x2 * cos], axis=-1)


def diff_attend(q, k, v, lam):
    s = jnp.einsum('bqhmd,bkhmd->bhmqk', q, k,
                   preferred_element_type=jnp.float32) * (DA_QK_DIM ** -0.5)
    p = jax.nn.softmax(s, axis=-1)
    a = p[:, :, 0] - lam * p[:, :, 1]
    return jnp.einsum('bhqk,bkhd->bqhd', a.astype(v.dtype), v)


def diff_attn_mixer(hx, hc, w_in, q_gain, k_gain, lam_qk, sub_gain, w_out, layer_idx, need_ctx):
    B, S, _ = hx.shape
    lam_init = 0.8 - 0.6 * math.exp(-0.3 * layer_idx)
    lq = lam_qk.astype(jnp.float32)
    lam = jnp.exp(jnp.sum(lq[0] * lq[1])) - jnp.exp(jnp.sum(lq[2] * lq[3])) + lam_init

    def project(h):
        n = h.shape[1]
        q, k, v = jnp.split(h @ w_in, 3, axis=-1)
        q = rms_norm(q.reshape(B, n, DA_HEADS, 2, DA_QK_DIM), q_gain)
        k = rms_norm(k.reshape(B, n, DA_HEADS, 2, DA_QK_DIM), k_gain)
        return q, k, v.reshape(B, n, DA_HEADS, DA_V_DIM)

    qx, kx, vx = project(hx)
    qc, kc, vc = project(hc)
    cos, sin = axial_rope_tables(S)
    qx = apply_rope(qx, cos, sin)
    kx = apply_rope(kx, cos, sin)
    k_all = jnp.concatenate([kc, kx], axis=1)
    v_all = jnp.concatenate([vc, vx], axis=1)
    nb = S // Q_BLOCK
    q_blocks = qx.reshape(B, nb, Q_BLOCK, DA_HEADS, 2, DA_QK_DIM).transpose(1, 0, 2, 3, 4, 5)
    o = lax.map(lambda qb: diff_attend(qb, k_all, v_all, lam), q_blocks)
    ox = o.transpose(1, 0, 2, 3, 4).reshape(B, S, DA_HEADS, DA_V_DIM)

    def finish(o):
        n = o.shape[1]
        o = rms_norm(o, sub_gain) * (1.0 - lam_init)
        return o.reshape(B, n, DA_HEADS * DA_V_DIM) @ w_out

    yx = finish(ox)
    yc = finish(diff_attend(qc, kc, vc, lam)) if need_ctx else None
    return yx, yc


def dw_conv_centred(x, w, b):
    n = x.shape[1]
    left = CONV_W // 2
    xp = jnp.pad(x, ((0, 0), (left, CONV_W - 1 - left), (0, 0)))
    return b + sum(w[k] * xp[:, k:k + n] for k in range(CONV_W))


def block_diag(x, w, b):
    B, n, _ = x.shape
    xb = x.reshape(B, n, LRU_BLOCKS, LRU_BLOCK_DIM)
    return jnp.einsum('bngi,gij->bngj', xb, w).reshape(B, n, LRU_WIDTH) + b


def lru_coeffs(xc, w_a, b_a, w_x, b_x, lam, reset_idx):
    r = jax.nn.sigmoid(block_diag(xc, w_a, b_a)).astype(jnp.float32)
    i = jax.nn.sigmoid(block_diag(xc, w_x, b_x)).astype(jnp.float32)
    log_a = -LRU_C * r * jax.nn.softplus(-lam.astype(jnp.float32))
    a = jnp.exp(log_a)
    mult = jnp.sqrt(-jnp.expm1(2.0 * log_a))
    if reset_idx is not None:
        mult = mult.at[:, reset_idx].set(1.0)
    return a, mult * i * xc.astype(jnp.float32)


def linear_scan(a, b, h0, reverse):
    if h0 is not None:
        edge = -1 if reverse else 0
        b = b.at[:, edge].add(a[:, edge] * h0)

    def comb(e1, e2):
        a1, b1 = e1
        a2, b2 = e2
        return a1 * a2, a2 * b1 + b2

    _, h = lax.associative_scan(comb, (a, b), axis=1, reverse=reverse)
    return h


def rglru_mixer(hx, hc, w_in, conv_w, conv_b, w_a, b_a, w_x, b_x, lam, w_out, need_ctx):
    def branches(h):
        gate, xr = jnp.split(h @ w_in, 2, axis=-1)
        return gate, dw_conv_centred(xr, conv_w, conv_b)

    gx, xx = branches(hx)
    gc, xcx = branches(hc)
    hsum_x = 0.0
    hsum_c = 0.0
    for d, reverse in enumerate((False, True)):
        start = -1 if reverse else 0
        end = 0 if reverse else -1
        ac, bc = lru_coeffs(xcx, w_a[d], b_a[d], w_x[d], b_x[d], lam[d], start)
        hc_seq = linear_scan(ac, bc, None, reverse)
        h_final = hc_seq[:, end]
        ax, bx = lru_coeffs(xx, w_a[d], b_a[d], w_x[d], b_x[d], lam[d], None)
        hx_seq = linear_scan(ax, bx, h_final, reverse)
        hsum_x = hsum_x + hx_seq
        hsum_c = hsum_c + hc_seq
    yx = (hsum_x.astype(gx.dtype) * jax.nn.gelu(gx)) @ w_out
    yc = (hsum_c.astype(gc.dtype) * jax.nn.gelu(gc)) @ w_out if need_ctx else None
    return yx, yc


def centred_mean_minus_self(x, window):
    B, n, C = x.shape
    cs = jnp.concatenate([jnp.zeros((B, 1, C), x.dtype), jnp.cumsum(x, axis=1)], axis=1)
    t = jnp.arange(n)
    lo = jnp.clip(t - window // 2, 0, n)
    hi = jnp.clip(t + window // 2, 0, n)
    cnt = (hi - lo).astype(jnp.float32)[None, :, None]
    return (cs[:, hi] - cs[:, lo]) / cnt - x


def pool_mixer(hx, hc, w_in, w_grp, scale, need_ctx):
    def mix(h):
        B, n, _ = h.shape
        u = (h @ w_in).astype(jnp.float32).reshape(B, n, POOL_GROUPS, POOL_GROUP_DIM)
        pooled = jnp.stack([centred_mean_minus_self(u[:, :, g], w)
                            for g, w in enumerate(POOL_WINDOWS)], axis=2)
        y = jnp.einsum('bngc,gcd->bngd', pooled.astype(h.dtype), w_grp).reshape(B, n, D_MODEL)
        return y * scale

    return mix(hx), (mix(hc) if need_ctx else None)


def hier_moe(h, router_g, router_e, w1, w3, w2):
    B, n, D = h.shape
    t = h.reshape(B * n, D)
    lg = (t @ router_g).astype(jnp.float32)
    pg = jax.nn.softmax(lg, axis=-1)
    g_sel = jnp.argmax(lg, axis=-1)
    p_sel = jnp.take_along_axis(pg, g_sel[:, None], axis=-1)
    le = (t @ router_e).astype(jnp.float32).reshape(-1, N_GROUPS, EXPERTS_PER_GROUP)
    le_sel = jnp.take_along_axis(le, g_sel[:, None, None], axis=1)[:, 0]
    top_p, top_i = lax.top_k(jax.nn.softmax(le_sel, axis=-1), TOP_K)
    top_p = top_p / jnp.sum(top_p, axis=-1, keepdims=True) * p_sel
    expert_id = g_sel[:, None] * EXPERTS_PER_GROUP + top_i
    combine = jnp.sum(jax.nn.one_hot(expert_id, N_EXPERTS, dtype=jnp.float32)
                      * top_p[..., None], axis=1)
    y = jnp.zeros((B * n, D), jnp.float32)
    for e in range(N_EXPERTS):
        he = jax.nn.silu(t @ w1[e]) * (t @ w3[e])
        y = y + combine[:, e:e + 1] * (he @ w2[e]).astype(jnp.float32)
    return y.astype(h.dtype).reshape(B, n, D)


def setup_inputs(seed: int = 0) -> dict:
    key = jax.random.key(seed)
    ks = iter(jax.random.split(key, 40))
    f32 = jnp.float32
    D = D_MODEL

    def nrm(shape, fan_in):
        return jax.random.normal(next(ks), shape, f32) * fan_in ** -0.5

    def gain(shape):
        return 1.0 + 0.1 * jax.random.normal(next(ks), shape, f32)

    def small(shape):
        return 0.02 * jax.random.normal(next(ks), shape, f32)

    u = jax.random.uniform(next(ks), (N_LAYERS_B, 2, LRU_WIDTH), f32, 0.9, 0.999)
    s = u ** (1.0 / LRU_C)
    lru_lam = jnp.log(s) - jnp.log1p(-s)
    return {
        "x": jax.random.normal(next(ks), (BATCH, SEQ, D), f32),
        "c": jax.random.normal(next(ks), (BATCH, D), f32),
        "ctx": jax.random.normal(next(ks), (BATCH, CTX_LEN, D), f32),
        "c_ctx": jax.random.normal(next(ks), (D,), f32),
        "w_ada": nrm((DEPTH, D, 6 * D), D),
        "b_ada": small((DEPTH, 6 * D)),
        "norm_g": gain((DEPTH, 2, D)),
        "attn_w_in": nrm((N_LAYERS_A, D, 3 * D), D),
        "attn_q_gain": gain((N_LAYERS_A, DA_QK_DIM)),
        "attn_k_gain": gain((N_LAYERS_A, DA_QK_DIM)),
        "attn_lam": 0.1 * jax.random.normal(next(ks), (N_LAYERS_A, 4, DA_QK_DIM), f32),
        "attn_sub_gain": gain((N_LAYERS_A, DA_V_DIM)),
        "attn_w_out": nrm((N_LAYERS_A, DA_HEADS * DA_V_DIM, D), DA_HEADS * DA_V_DIM),
        "lru_w_in": nrm((N_LAYERS_B, D, 2 * LRU_WIDTH), D),
        "lru_conv_w": nrm((N_LAYERS_B, CONV_W, LRU_WIDTH), CONV_W),
        "lru_conv_b": small((N_LAYERS_B, LRU_WIDTH)),
        "lru_w_a": nrm((N_LAYERS_B, 2, LRU_BLOCKS, LRU_BLOCK_DIM, LRU_BLOCK_DIM), LRU_BLOCK_DIM),
        "lru_b_a": small((N_LAYERS_B, 2, LRU_WIDTH)),
        "lru_w_x": nrm((N_LAYERS_B, 2, LRU_BLOCKS, LRU_BLOCK_DIM, LRU_BLOCK_DIM), LRU_BLOCK_DIM),
        "lru_b_x": small((N_LAYERS_B, 2, LRU_WIDTH)),
        "lru_lam": lru_lam,
        "lru_w_out": nrm((N_LAYERS_B, LRU_WIDTH, D), LRU_WIDTH),
        "pool_w_in": nrm((N_LAYERS_C, D, D), D),
        "pool_w_grp": nrm((N_LAYERS_C, POOL_GROUPS, POOL_GROUP_DIM, POOL_GROUP_DIM), POOL_GROUP_DIM),
        "pool_scale": gain((N_LAYERS_C, D)),
        "moe_router_g": nrm((DEPTH, D, N_GROUPS), D),
        "moe_router_e": nrm((DEPTH, D, N_EXPERTS), D),
        "moe_w1": nrm((DEPTH, N_EXPERTS, D, D_EXPERT), D),
        "moe_w3": nrm((DEPTH, N_EXPERTS, D, D_EXPERT), D),
        "moe_w2": nrm((DEPTH, N_EXPERTS, D_EXPERT, D), D_EXPERT),
    }


def reference(x, c, ctx, c_ctx, w_ada, b_ada, norm_g,
              attn_w_in, attn_q_gain, attn_k_gain, attn_lam, attn_sub_gain, attn_w_out,
              lru_w_in, lru_conv_w, lru_conv_b, lru_w_a, lru_b_a, lru_w_x, lru_b_x, lru_lam, lru_w_out,
              pool_w_in, pool_w_grp, pool_scale,
              moe_router_g, moe_router_e, moe_w1, moe_w3, moe_w2):
    xs = x
    cs = ctx
    sc = jax.nn.silu(c)
    scc = jax.nn.silu(c_ctx)
    for i in range(DEPTH):
        need_ctx = i < DEPTH - 1
        mx = (sc @ w_ada[i] + b_ada[i])[:, None, :]
        mc = (scc @ w_ada[i] + b_ada[i])[None, None, :]
        sh1x, sc1x, g1x, sh2x, sc2x, g2x = jnp.split(mx, 6, axis=-1)
        sh1c, sc1c, g1c, sh2c, sc2c, g2c = jnp.split(mc, 6, axis=-1)

        ux = modulate(rms_norm(xs, norm_g[i, 0]), sh1x, sc1x)
        uc = modulate(rms_norm(cs, norm_g[i, 0]), sh1c, sc1c)
        kind, j = i % N_MIXERS, i // N_MIXERS
        if kind == 0:
            yx, yc = diff_attn_mixer(ux, uc, attn_w_in[j], attn_q_gain[j], attn_k_gain[j],
                                     attn_lam[j], attn_sub_gain[j], attn_w_out[j], i, need_ctx)
        elif kind == 1:
            yx, yc = rglru_mixer(ux, uc, lru_w_in[j], lru_conv_w[j], lru_conv_b[j], lru_w_a[j],
                                 lru_b_a[j], lru_w_x[j], lru_b_x[j], lru_lam[j], lru_w_out[j], need_ctx)
        else:
            yx, yc = pool_mixer(ux, uc, pool_w_in[j], pool_w_grp[j], pool_scale[j], need_ctx)
        xs = xs + g1x * yx
        ux = modulate(rms_norm(xs, norm_g[i, 1]), sh2x, sc2x)
        xs = xs + g2x * hier_moe(ux, moe_router_g[i], moe_router_e[i], moe_w1[i], moe_w3[i], moe_w2[i])
        if need_ctx:
            cs = cs + g1c * yc
            uc = modulate(rms_norm(cs, norm_g[i, 1]), sh2c, sc2c)
            cs = cs + g2c * hier_moe(uc, moe_router_g[i], moe_router_e[i], moe_w1[i], moe_w3[i], moe_w2[i])
    return xs
```

```python
import functools
import math

import jax
import jax.numpy as jnp
from jax import lax
from jax.experimental import pallas as pl
from jax.experimental.pallas import tpu as pltpu

F32 = jnp.float32
BF16 = jnp.bfloat16

EPS = 1e-6
GRID_W = 64
ROPE_BASE = 10000.0
LRU_C = 8.0
POOL_WINDOWS = (2, 4, 8, 16)
EXPERTS_PER_GROUP = 4
LANES = 128
VMEM_LIMIT = 56 << 20
ROW_CHUNK = 512


def _cparams(sem):
    return pltpu.CompilerParams(dimension_semantics=sem, vmem_limit_bytes=VMEM_LIMIT)


def _norm_mod(x, g, shift, scale):
    ms = jnp.mean(x * x, axis=-1, keepdims=True)
    return (x * lax.rsqrt(ms + EPS) * g) * (1.0 + scale) + shift


def _bidx(bm):
    return (lambda b: b) if bm > 1 else (lambda b: 0)


def _ada_kernel(c_ref, w_ref, b_ref, o_ref):
    c = c_ref[...]
    sc = c * jax.nn.sigmoid(c)
    o_ref[0] = jnp.dot(sc, w_ref[0], preferred_element_type=F32,
                       precision=lax.Precision.HIGHEST) + b_ref[0]


def _ada(cc, w_ada, b_ada):
    depth, d, n6 = w_ada.shape
    r = cc.shape[0]
    tn = 1536
    return pl.pallas_call(
        _ada_kernel,
        grid=(depth, n6 // tn),
        in_specs=[pl.BlockSpec((r, d), lambda i, j: (0, 0)),
                  pl.BlockSpec((1, d, tn), lambda i, j: (i, 0, j)),
                  pl.BlockSpec((1, 1, tn), lambda i, j: (i, 0, j))],
        out_specs=pl.BlockSpec((1, r, tn), lambda i, j: (i, 0, j)),
        out_shape=jax.ShapeDtypeStruct((depth, r, n6), F32),
        compiler_params=_cparams(("parallel", "parallel")),
        name="ada",
    )(cc, w_ada, b_ada.reshape(depth, 1, n6))


def _nmm_kernel(*refs, epilogue, n_extra):
    x_ref, g_ref, sh_ref, sc_ref, w_ref = refs[:5]
    extra = refs[5:5 + n_extra]
    outs = refs[5 + n_extra:-1]
    u_ref = refs[-1]
    j = pl.program_id(1)
    n = x_ref.shape[1]

    @pl.when(j == 0)
    def _():
        for r0 in range(0, n, ROW_CHUNK):
            r1 = min(n, r0 + ROW_CHUNK)
            u_ref[r0:r1, :] = _norm_mod(x_ref[0, r0:r1, :], g_ref[...], sh_ref[0], sc_ref[0]).astype(BF16)

    acc = jnp.dot(u_ref[...], w_ref[...], preferred_element_type=F32)
    epilogue(acc, j, extra, outs)


def _nmm(xs, g, shift, scale, w, tn, epilogue, extra, extra_specs, out_shapes, out_specs, name):
    b, n, d = xs.shape
    bi = _bidx(shift.shape[0])
    nout = w.shape[1]
    kern = functools.partial(_nmm_kernel, epilogue=epilogue, n_extra=len(extra))
    return pl.pallas_call(
        kern,
        grid=(b, nout // tn),
        in_specs=[pl.BlockSpec((1, n, d), lambda i, j: (i, 0, 0)),
                  pl.BlockSpec((1, d), lambda i, j: (0, 0)),
                  pl.BlockSpec((1, 1, d), lambda i, j: (bi(i), 0, 0)),
                  pl.BlockSpec((1, 1, d), lambda i, j: (bi(i), 0, 0)),
                  pl.BlockSpec((d, tn), lambda i, j: (0, j))] + list(extra_specs),
        out_specs=out_specs,
        out_shape=out_shapes,
        scratch_shapes=[pltpu.VMEM((n, d), BF16)],
        compiler_params=_cparams(("parallel", "arbitrary")),
        name=name,
    )(xs, g, shift, scale, w, *extra)


def _qkv_epilogue(acc, j, extra, outs, *, use_rope, n_sec, dqk):
    gain_ref, cos_ref, sin_ref, gmat_ref = extra
    o_ref = outs[0]
    tn = acc.shape[1]

    @pl.when(j < 2 * n_sec)
    def _():
        for hb in range(tn // LANES):
            a = acc[:, hb * LANES:(hb + 1) * LANES]
            ss = jnp.dot((a * a).astype(BF16), gmat_ref[...], preferred_element_type=F32)
            y = a * lax.rsqrt(ss * (1.0 / dqk) + EPS) * gain_ref[0]
            if use_rope:
                lane = lax.broadcasted_iota(jnp.int32, y.shape, 1)
                half = dqk // 2
                partner = jnp.where(lane % dqk < half,
                                    pltpu.roll(y, LANES - half, axis=1),
                                    pltpu.roll(y, half, axis=1))
                y = y * cos_ref[...] + partner * sin_ref[...]
            o_ref[0, :, hb * LANES:(hb + 1) * LANES] = y.astype(BF16)

    @pl.when(j >= 2 * n_sec)
    def _():
        o_ref[0] = acc.astype(BF16)


def _qkv_proj(xs, g, shift, scale, w, gains, cos, sin, gmat, use_rope, dqk, name):
    b, n, d = xs.shape
    tn = 256
    n_sec = d // tn
    epi = functools.partial(_qkv_epilogue, use_rope=use_rope, n_sec=n_sec, dqk=dqk)
    extra = (gains, cos, sin, gmat)
    extra_specs = [pl.BlockSpec((1, 1, LANES), lambda i, j: (jnp.minimum(j // n_sec, 1), 0, 0)),
                   pl.BlockSpec((n, LANES), lambda i, j: (0, 0)),
                   pl.BlockSpec((n, LANES), lambda i, j: (0, 0)),
                   pl.BlockSpec((LANES, LANES), lambda i, j: (0, 0))]
    return _nmm(xs, g, shift, scale, w, tn, epi, extra, extra_specs,
                jax.ShapeDtypeStruct((b, n, 3 * d), BF16),
                pl.BlockSpec((1, n, tn), lambda i, j: (i, 0, j)), name)


def _attn_kernel(lam_ref, q_ref, *refs, n_parts, dqk, post_scale):
    k_refs = refs[:n_parts]
    v_refs = refs[n_parts:2 * n_parts]
    sg_ref = refs[2 * n_parts]
    o_ref = refs[2 * n_parts + 1]
    lam = lam_ref[0]
    q = q_ref[0]
    lane = lax.broadcasted_iota(jnp.int32, q.shape, 1)
    zero = jnp.zeros_like(q)
    qm = (jnp.where(lane < dqk, q, zero), jnp.where(lane >= dqk, q, zero))
    dn = (((1,), (1,)), ((), ()))
    es, inv = [], []
    for m in range(2):
        s = [lax.dot_general(qm[m], k_ref[0], dn, preferred_element_type=F32) for k_ref in k_refs]
        mx = functools.reduce(jnp.maximum, [jnp.max(sp, axis=-1, keepdims=True) for sp in s])
        e = [jnp.exp(sp - mx) for sp in s]
        den = functools.reduce(lambda a, c: a + c, [jnp.sum(ep, axis=-1, keepdims=True) for ep in e])
        es.append(e)
        inv.append(1.0 / den)
    c0 = inv[0]
    c1 = lam * inv[1]
    o = None
    for p in range(n_parts):
        a = (es[0][p] * c0 - es[1][p] * c1).astype(BF16)
        t = jnp.dot(a, v_refs[p][0], preferred_element_type=F32)
        o = t if o is None else o + t
    ms = jnp.mean(o * o, axis=-1, keepdims=True)
    o = o * lax.rsqrt(ms + EPS) * (sg_ref[...] * post_scale)
    o_ref[0] = o.astype(BF16)


def _attention(lam, q_arr, kv_arrs, sub_gain, heads, dqk, post_scale, name):
    b, nq, d3 = q_arr.shape
    d = d3 // 3
    dv = 2 * dqk
    tq = min(nq, 256)
    n_parts = len(kv_arrs)
    kern = functools.partial(_attn_kernel, n_parts=n_parts, dqk=dqk, post_scale=post_scale)
    in_specs = [pl.BlockSpec(memory_space=pltpu.SMEM),
                pl.BlockSpec((1, tq, dv), lambda i, h, t: (i, t, h))]
    in_specs += [pl.BlockSpec((1, a.shape[1], dv), lambda i, h, t: (i, 0, heads + h)) for a in kv_arrs]
    in_specs += [pl.BlockSpec((1, a.shape[1], dv), lambda i, h, t: (i, 0, 2 * heads + h)) for a in kv_arrs]
    in_specs += [pl.BlockSpec((1, dv), lambda i, h, t: (0, 0))]
    return pl.pallas_call(
        kern,
        grid=(b, heads, nq // tq),
        in_specs=in_specs,
        out_specs=pl.BlockSpec((1, tq, dv), lambda i, h, t: (i, t, h)),
        out_shape=jax.ShapeDtypeStruct((b, nq, d), BF16),
        compiler_params=_cparams(("parallel", "parallel", "arbitrary")),
        name=name,
    )(lam, q_arr, *kv_arrs, *kv_arrs, sub_gain)


def _mm_res_kernel(a_ref, w_ref, x_ref, g_ref, o_ref):
    y = jnp.dot(a_ref[0], w_ref[...], preferred_element_type=F32)
    o_ref[0] = x_ref[0] + g_ref[0] * y


def _mm_res(a, w, xs, gate, name):
    b, n, k = a.shape
    d = w.shape[1]
    tm = min(n, 512)
    bi = _bidx(gate.shape[0])
    return pl.pallas_call(
        _mm_res_kernel,
        grid=(b, n // tm),
        in_specs=[pl.BlockSpec((1, tm, k), lambda i, t: (i, t, 0)),
                  pl.BlockSpec((k, d), lambda i, t: (0, 0)),
                  pl.BlockSpec((1, tm, d), lambda i, t: (i, t, 0)),
                  pl.BlockSpec((1, 1, d), lambda i, t: (bi(i), 0, 0))],
        out_specs=pl.BlockSpec((1, tm, d), lambda i, t: (i, t, 0)),
        out_shape=jax.ShapeDtypeStruct((b, n, d), F32),
        compiler_params=_cparams(("parallel", "parallel")),
        name=name,
    )(a, w, xs, gate)


def _gelu_epilogue(acc, j, extra, outs):
    outs[0][0] = jax.nn.gelu(acc).astype(BF16)


def _shift_rows(x, s):
    n = x.shape[0]
    row = lax.broadcasted_iota(jnp.int32, x.shape, 0)
    r = pltpu.roll(x, s % n, axis=0)
    if s >= 0:
        return jnp.where(row >= s, r, 0.0)
    return jnp.where(row < n + s, r, 0.0)


def _conv_epilogue(acc, j, extra, outs):
    cw_ref, cb_ref = extra
    width = cw_ref.shape[0]
    left = width // 2
    y = cb_ref[...] + jnp.zeros_like(acc)
    for k in range(width):
        y = y + cw_ref[k:k + 1, :] * _shift_rows(acc, left - k)
    outs[0][0] = y


def _lru_proj(xs, g, shift, scale, w_gate, w_xr, conv_w, conv_b, name):
    b, n, d = xs.shape
    wd = w_gate.shape[1]
    tn = 256
    gg = _nmm(xs, g, shift, scale, w_gate, tn, _gelu_epilogue, (), [],
              jax.ShapeDtypeStruct((b, n, wd), BF16),
              pl.BlockSpec((1, n, tn), lambda i, j: (i, 0, j)), name + "_gate")
    xc = _nmm(xs, g, shift, scale, w_xr, tn, _conv_epilogue, (conv_w, conv_b),
              [pl.BlockSpec((conv_w.shape[0], tn), lambda i, j: (0, j)),
               pl.BlockSpec((1, tn), lambda i, j: (0, j))],
              jax.ShapeDtypeStruct((b, n, wd), F32),
              pl.BlockSpec((1, n, tn), lambda i, j: (i, 0, j)), name + "_conv")
    return gg, xc


def _lru_scan_kernel(*refs, reverse, reset, final, tt):
    if final:
        xc_ref, wab_ref, bias_ref, lam_ref, h0_ref, prev_ref, gg_ref, o_ref, hfin_ref, a_s, b_s, h_s, st = refs
    else:
        xc_ref, wab_ref, bias_ref, lam_ref, h0_ref, o_ref, hfin_ref, a_s, b_s, h_s, st = refs
    i = pl.program_id(0)
    nb = xc_ref.shape[0]
    ng = wab_ref.shape[0]
    rows = nb * tt

    @pl.when(i == 0)
    def _():
        st[...] = h0_ref[...]

    first_t = tt - 1 if reverse else 0
    row = lax.broadcasted_iota(jnp.int32, (rows, LANES), 0)
    for gi in range(ng):
        xg = xc_ref[:, :, gi * LANES:(gi + 1) * LANES].reshape(rows, LANES)
        pre = jnp.dot(xg.astype(BF16), wab_ref[gi], preferred_element_type=F32) + bias_ref[gi]
        r = jax.nn.sigmoid(pre[:, :LANES])
        ig = jax.nn.sigmoid(pre[:, LANES:])
        sp = jax.nn.softplus(-lam_ref[gi])
        log_a = (-LRU_C) * r * sp
        a = jnp.exp(log_a)
        mult = jnp.sqrt(1.0 - a * a)
        if reset:
            mult = jnp.where(jnp.logical_and(row % tt == first_t, i == 0), 1.0, mult)
        a_s[gi] = a
        b_s[gi] = mult * ig * xg

    def step(s, hs):
        t = (tt - 1 - s) if reverse else s
        out = []
        for gi in range(ng):
            a = a_s[gi, pl.ds(t, nb, stride=tt), :]
            bb = b_s[gi, pl.ds(t, nb, stride=tt), :]
            h = a * hs[gi] + bb
            h_s[gi, pl.ds(t, nb, stride=tt), :] = h
            out.append(h)
        return tuple(out)

    hs = lax.fori_loop(0, tt, step, tuple(st[gi] for gi in range(ng)))
    for gi in range(ng):
        st[gi] = hs[gi]
        hfin_ref[gi] = hs[gi]
        hseq = h_s[gi].reshape(nb, tt, LANES)
        sl = slice(gi * LANES, (gi + 1) * LANES)
        if final:
            o_ref[:, :, sl] = ((prev_ref[:, :, sl] + hseq) * gg_ref[:, :, sl].astype(F32)).astype(BF16)
        else:
            o_ref[:, :, sl] = hseq


def _lru_scan(xc, wab, bias, lam, h0, prev, gg, reverse, reset, name):
    b, n, wd = xc.shape
    ng = wab.shape[0]
    tt = min(n, 32)
    nt = n // tt
    final = prev is not None
    tmap = (lambda i: (0, nt - 1 - i, 0)) if reverse else (lambda i: (0, i, 0))
    full3 = lambda i: (0, 0, 0)
    seq_spec = pl.BlockSpec((b, tt, wd), tmap)
    in_specs = [seq_spec,
                pl.BlockSpec(wab.shape, full3),
                pl.BlockSpec(bias.shape, full3),
                pl.BlockSpec(lam.shape, full3),
                pl.BlockSpec(h0.shape, full3)]
    args = [xc, wab, bias, lam, h0]
    if final:
        in_specs += [seq_spec, seq_spec]
        args += [prev, gg]
    kern = functools.partial(_lru_scan_kernel, reverse=reverse, reset=reset, final=final, tt=tt)
    return pl.pallas_call(
        kern,
        grid=(nt,),
        in_specs=in_specs,
        out_specs=[seq_spec, pl.BlockSpec(h0.shape, full3)],
        out_shape=[jax.ShapeDtypeStruct((b, n, wd), BF16 if final else F32),
                   jax.ShapeDtypeStruct(h0.shape, F32)],
        scratch_shapes=[pltpu.VMEM((ng, b * tt, LANES), F32),
                        pltpu.VMEM((ng, b * tt, LANES), F32),
                        pltpu.VMEM((ng, b * tt, LANES), F32),
                        pltpu.VMEM(h0.shape, F32)],
        compiler_params=_cparams(("arbitrary",)),
        name=name,
    )(*args)


def _window_mean_minus_self(x, window):
    n = x.shape[0]
    back, fwd = x, x
    span = 1
    while span < window // 2:
        back = back + _shift_rows(back, span)
        fwd = fwd + _shift_rows(fwd, -span)
        span *= 2
    p = _shift_rows(back, 1) + fwd
    t = lax.broadcasted_iota(jnp.int32, x.shape, 0)
    lo = jnp.maximum(t - window // 2, 0)
    hi = jnp.minimum(t + window // 2, n)
    cnt = (hi - lo).astype(F32)
    return p / cnt - x


def _pool_epilogue(acc, j, extra, outs):
    wg_ref, ps_ref, xcol_ref, gate_ref = extra
    o_ref = outs[0]
    for gidx, window in enumerate(POOL_WINDOWS):
        @pl.when(j == gidx)
        def _(window=window):
            pooled = _window_mean_minus_self(acc, window).astype(BF16)
            y = jnp.dot(pooled, wg_ref[0], preferred_element_type=F32) * ps_ref[...]
            o_ref[0] = xcol_ref[0] + gate_ref[0] * y


def _pool_mixer(xs, g, shift, scale, w_in, w_grp, pscale, gate, name):
    b, n, d = xs.shape
    ngrp, gd, _ = w_grp.shape
    bi = _bidx(gate.shape[0])
    extra = (w_grp, pscale, xs, gate)
    extra_specs = [pl.BlockSpec((1, gd, gd), lambda i, j: (j, 0, 0)),
                   pl.BlockSpec((1, gd), lambda i, j: (0, j)),
                   pl.BlockSpec((1, n, gd), lambda i, j: (i, 0, j)),
                   pl.BlockSpec((1, 1, gd), lambda i, j: (bi(i), 0, j))]
    return _nmm(xs, g, shift, scale, w_in, gd, _pool_epilogue, extra, extra_specs,
                jax.ShapeDtypeStruct((b, n, d), F32),
                pl.BlockSpec((1, n, gd), lambda i, j: (i, 0, j)), name)


def _route_kernel(x_ref, g_ref, sh_ref, sc_ref, rhi_ref, rlo_ref, u_ref, comb_ref, *, n_groups):
    u = _norm_mod(x_ref[0], g_ref[...], sh_ref[0], sc_ref[0])
    u_hi = u.astype(BF16)
    u_lo = (u - u_hi.astype(F32)).astype(BF16)
    u_ref[0] = u_hi
    lg = (jnp.dot(u_hi, rhi_ref[...], preferred_element_type=F32)
          + jnp.dot(u_lo, rhi_ref[...], preferred_element_type=F32)
          + jnp.dot(u_hi, rlo_ref[...], preferred_element_type=F32))
    lane = lax.broadcasted_iota(jnp.int32, lg.shape, 1)
    neg = jnp.float32(-jnp.inf)
    big = jnp.int32(1 << 20)
    is_g = lane < n_groups
    gmax = jnp.max(jnp.where(is_g, lg, neg), axis=-1, keepdims=True)
    g_sel = jnp.min(jnp.where(jnp.logical_and(is_g, lg == gmax), lane, big), axis=-1, keepdims=True)
    p_sel = 1.0 / jnp.sum(jnp.where(is_g, jnp.exp(lg - gmax), 0.0), axis=-1, keepdims=True)
    e_idx = lane - n_groups
    in_grp = jnp.logical_and(e_idx >= g_sel * EXPERTS_PER_GROUP, e_idx < (g_sel + 1) * EXPERTS_PER_GROUP)
    m1 = jnp.max(jnp.where(in_grp, lg, neg), axis=-1, keepdims=True)
    i1 = jnp.min(jnp.where(jnp.logical_and(in_grp, lg == m1), lane, big), axis=-1, keepdims=True)
    rest = jnp.logical_and(in_grp, lane != i1)
    m2 = jnp.max(jnp.where(rest, lg, neg), axis=-1, keepdims=True)
    i2 = jnp.min(jnp.where(jnp.logical_and(rest, lg == m2), lane, big), axis=-1, keepdims=True)
    e2 = jnp.exp(m2 - m1)
    w1 = p_sel / (1.0 + e2)
    w2 = w1 * e2
    comb_ref[0] = jnp.where(lane == i1, w1, 0.0) + jnp.where(lane == i2, w2, 0.0)


def _route(xs, g, shift, scale, r_hi, r_lo, n_groups, name):
    b, n, d = xs.shape
    tm = min(n, 512)
    bi = _bidx(shift.shape[0])
    kern = functools.partial(_route_kernel, n_groups=n_groups)
    return pl.pallas_call(
        kern,
        grid=(b, n // tm),
        in_specs=[pl.BlockSpec((1, tm, d), lambda i, t: (i, t, 0)),
                  pl.BlockSpec((1, d), lambda i, t: (0, 0)),
                  pl.BlockSpec((1, 1, d), lambda i, t: (bi(i), 0, 0)),
                  pl.BlockSpec((1, 1, d), lambda i, t: (bi(i), 0, 0)),
                  pl.BlockSpec((d, LANES), lambda i, t: (0, 0)),
                  pl.BlockSpec((d, LANES), lambda i, t: (0, 0))],
        out_specs=[pl.BlockSpec((1, tm, d), lambda i, t: (i, t, 0)),
                   pl.BlockSpec((1, tm, LANES), lambda i, t: (i, t, 0))],
        out_shape=[jax.ShapeDtypeStruct((b, n, d), BF16),
                   jax.ShapeDtypeStruct((b, n, LANES), F32)],
        compiler_params=_cparams(("parallel", "parallel")),
        name=name,
    )(xs, g, shift, scale, r_hi, r_lo)


def _moe_kernel(u_ref, comb_ref, w1_ref, w3_ref, w2_ref, x_ref, g_ref, o_ref, acc_ref, *, n_groups):
    e = pl.program_id(2)

    @pl.when(e == 0)
    def _():
        acc_ref[...] = jnp.zeros_like(acc_ref)

    u = u_ref[0]
    h1 = jnp.dot(u, w1_ref[0], preferred_element_type=F32)
    h3 = jnp.dot(u, w3_ref[0], preferred_element_type=F32)
    he = (h1 * jax.nn.sigmoid(h1) * h3).astype(BF16)
    y = jnp.dot(he, w2_ref[0], preferred_element_type=F32)
    comb = comb_ref[0]
    lane = lax.broadcasted_iota(jnp.int32, comb.shape, 1)
    c = jnp.sum(jnp.where(lane == e + n_groups, comb, 0.0), axis=-1, keepdims=True)
    acc_ref[...] += c * y

    @pl.when(e == pl.num_programs(2) - 1)
    def _():
        o_ref[0] = x_ref[0] + g_ref[0] * acc_ref[...]


def _moe(u, comb, w1, w3, w2, xs, gate, n_groups, name):
    b, n, d = xs.shape
    ne, _, f = w1.shape
    tm = min(n, 1024)
    bi = _bidx(gate.shape[0])
    kern = functools.partial(_moe_kernel, n_groups=n_groups)
    return pl.pallas_call(
        kern,
        grid=(b, n // tm, ne),
        in_specs=[pl.BlockSpec((1, tm, d), lambda i, t, e: (i, t, 0)),
                  pl.BlockSpec((1, tm, LANES), lambda i, t, e: (i, t, 0)),
                  pl.BlockSpec((1, d, f), lambda i, t, e: (e, 0, 0)),
                  pl.BlockSpec((1, d, f), lambda i, t, e: (e, 0, 0)),
                  pl.BlockSpec((1, f, d), lambda i, t, e: (e, 0, 0)),
                  pl.BlockSpec((1, tm, d), lambda i, t, e: (i, t, 0)),
                  pl.BlockSpec((1, 1, d), lambda i, t, e: (bi(i), 0, 0))],
        out_specs=pl.BlockSpec((1, tm, d), lambda i, t, e: (i, t, 0)),
        out_shape=jax.ShapeDtypeStruct((b, n, d), F32),
        scratch_shapes=[pltpu.VMEM((tm, d), F32)],
        compiler_params=_cparams(("parallel", "parallel", "arbitrary")),
        name=name,
    )(u, comb, w1, w3, w2, xs, gate)


def _rope_tables(n, dqk):
    rows = n // GRID_W
    row = jnp.repeat(jnp.arange(rows), GRID_W).astype(F32)
    col = jnp.tile(jnp.arange(GRID_W), rows).astype(F32)
    n_freq = dqk // 4
    inv = ROPE_BASE ** (-jnp.arange(n_freq, dtype=F32) / n_freq)
    ang = jnp.concatenate([row[:, None] * inv, col[:, None] * inv], axis=-1)
    cos, sin = jnp.cos(ang), jnp.sin(ang)
    reps = LANES // dqk
    cos_l = jnp.tile(jnp.concatenate([cos, cos], axis=-1), (1, reps))
    sin_l = jnp.tile(jnp.concatenate([-sin, sin], axis=-1), (1, reps))
    return cos_l, sin_l


def kernel(x, c, ctx, c_ctx, w_ada, b_ada, norm_g, attn_w_in, attn_q_gain, attn_k_gain, attn_lam, attn_sub_gain, attn_w_out, lru_w_in, lru_conv_w, lru_conv_b, lru_w_a, lru_b_a, lru_w_x, lru_b_x, lru_lam, lru_w_out, pool_w_in, pool_w_grp, pool_scale, moe_router_g, moe_router_e, moe_w1, moe_w3, moe_w2):
    bsz, seq, d = x.shape
    n_ctx = ctx.shape[1]
    depth = w_ada.shape[0]
    dqk = attn_q_gain.shape[-1]
    heads = d // (2 * dqk)
    n_groups = moe_router_g.shape[-1]
    n_experts = moe_router_e.shape[-1]
    lru_w = lru_w_out.shape[1]
    lru_g = lru_w_a.shape[2]

    pad = (-(bsz + 1)) % 8
    cc = jnp.concatenate([c, c_ctx[None, :], jnp.zeros((pad, d), F32)], axis=0)
    mods = _ada(cc, w_ada, b_ada)

    cos_l, sin_l = _rope_tables(seq, dqk)
    cos_c = jnp.ones((n_ctx, LANES), F32)
    sin_c = jnp.zeros((n_ctx, LANES), F32)
    lane = jnp.arange(LANES)
    gmat = (lane[:, None] // dqk == lane[None, :] // dqk).astype(BF16)

    xs, cs = x, ctx
    for i in range(depth):
        need_ctx = i < depth - 1
        mx = [mods[i, :bsz, k * d:(k + 1) * d][:, None, :] for k in range(6)]
        mc = [mods[i, bsz:bsz + 1, k * d:(k + 1) * d][:, None, :] for k in range(6)]
        g1 = norm_g[i, 0][None, :]
        g2 = norm_g[i, 1][None, :]
        kind, j = i % 3, i // 3
        nm = f"l{i}"
        if kind == 0:
            lam_init = 0.8 - 0.6 * math.exp(-0.3 * i)
            lq = attn_lam[j].astype(F32)
            lam = (jnp.exp(jnp.sum(lq[0] * lq[1])) - jnp.exp(jnp.sum(lq[2] * lq[3])) + lam_init).reshape(1)
            w_in = attn_w_in[j].astype(BF16)
            reps = LANES // dqk
            gains = jnp.stack([jnp.tile(attn_q_gain[j], reps) * (dqk ** -0.5),
                               jnp.tile(attn_k_gain[j], reps)])[:, None, :]
            sub_gain = attn_sub_gain[j][None, :]
            qkv_x = _qkv_proj(xs, g1, mx[0], mx[1], w_in, gains, cos_l, sin_l, gmat, True, dqk, nm + "_qkv_x")
            qkv_c = _qkv_proj(cs, g1, mc[0], mc[1], w_in, gains, cos_c, sin_c, gmat, False, dqk, nm + "_qkv_c")
            w_out = attn_w_out[j].astype(BF16)
            ox = _attention(lam, qkv_x, [qkv_c, qkv_x], sub_gain, heads, dqk, 1.0 - lam_init, nm + "_attn_x")
            xs = _mm_res(ox, w_out, xs, mx[2], nm + "_out_x")
            if need_ctx:
                oc = _attention(lam, qkv_c, [qkv_c], sub_gain, heads, dqk, 1.0 - lam_init, nm + "_attn_c")
                cs = _mm_res(oc, w_out, cs, mc[2], nm + "_out_c")
        elif kind == 1:
            w_in = lru_w_in[j].astype(BF16)
            w_gate, w_xr = w_in[:, :lru_w], w_in[:, lru_w:]
            conv_b = lru_conv_b[j][None, :]
            ggx, xcx = _lru_proj(xs, g1, mx[0], mx[1], w_gate, w_xr, lru_conv_w[j], conv_b, nm + "_proj_x")
            ggc, xcc = _lru_proj(cs, g1, mc[0], mc[1], w_gate, w_xr, lru_conv_w[j], conv_b, nm + "_proj_c")
            h0 = jnp.zeros((lru_g, bsz, LANES), F32)
            hx_prev, hc_prev = None, None
            for dr, reverse in enumerate((False, True)):
                wab = jnp.concatenate([lru_w_a[j, dr], lru_w_x[j, dr]], axis=-1).astype(BF16)
                bias = jnp.concatenate([lru_b_a[j, dr].reshape(lru_g, 1, LANES),
                                        lru_b_x[j, dr].reshape(lru_g, 1, LANES)], axis=-1)
                lam_d = lru_lam[j, dr].reshape(lru_g, 1, LANES)
                last = dr == 1
                hc_seq, hc_fin = _lru_scan(xcc, wab, bias, lam_d, h0, hc_prev if last else None,
                                           ggc if last else None, reverse, True, f"{nm}_scan_c{dr}")
                hx_seq, _ = _lru_scan(xcx, wab, bias, lam_d, hc_fin, hx_prev if last else None,
                                      ggx if last else None, reverse, False, f"{nm}_scan_x{dr}")
                hx_prev, hc_prev = hx_seq, hc_seq
            w_out = lru_w_out[j].astype(BF16)
            xs = _mm_res(hx_prev, w_out, xs, mx[2], nm + "_out_x")
            if need_ctx:
                cs = _mm_res(hc_prev, w_out, cs, mc[2], nm + "_out_c")
        else:
            w_in = pool_w_in[j].astype(BF16)
            w_grp = pool_w_grp[j].astype(BF16)
            pscale = pool_scale[j][None, :]
            xs_new = _pool_mixer(xs, g1, mx[0], mx[1], w_in, w_grp, pscale, mx[2], nm + "_pool_x")
            if need_ctx:
                cs = _pool_mixer(cs, g1, mc[0], mc[1], w_in, w_grp, pscale, mc[2], nm + "_pool_c")
            xs = xs_new

        r_all = jnp.concatenate([moe_router_g[i], moe_router_e[i],
                                 jnp.zeros((d, LANES - n_groups - n_experts), F32)], axis=-1)
        r_hi = r_all.astype(BF16)
        r_lo = (r_all - r_hi.astype(F32)).astype(BF16)
        w1 = moe_w1[i].astype(BF16)
        w3 = moe_w3[i].astype(BF16)
        w2 = moe_w2[i].astype(BF16)
        ux, combx = _route(xs, g2, mx[3], mx[4], r_hi, r_lo, n_groups, nm + "_route_x")
        xs = _moe(ux, combx, w1, w3, w2, xs, mx[5], n_groups, nm + "_moe_x")
        if need_ctx:
            uc, combc = _route(cs, g2, mc[3], mc[4], r_hi, r_lo, n_groups, nm + "_route_c")
            cs = _moe(uc, combc, w1, w3, w2, cs, mc[5], n_groups, nm + "_moe_c")
    return xs
```

```python
import functools
import math

import jax
import jax.numpy as jnp
from jax import lax
from jax.experimental import pallas as pl
from jax.experimental.pallas import tpu as pltpu

F32 = jnp.float32
BF16 = jnp.bfloat16

EPS = 1e-6
GRID_W = 64
ROPE_BASE = 10000.0
LRU_C = 8.0
POOL_WINDOWS = (2, 4, 8, 16)
EXPERTS_PER_GROUP = 4
PAIRS_PER_GROUP = EXPERTS_PER_GROUP * (EXPERTS_PER_GROUP - 1) // 2
ROUTE_BLOCK = 512
MOE_TILE = 512
DMA_RING = 256
LANES = 128
VMEM_LIMIT = 56 << 20
ROW_CHUNK = 512


def _cparams(sem):
    return pltpu.CompilerParams(dimension_semantics=sem, vmem_limit_bytes=VMEM_LIMIT)


def _norm_mod(x, g, shift, scale):
    ms = jnp.mean(x * x, axis=-1, keepdims=True)
    return (x * lax.rsqrt(ms + EPS) * g) * (1.0 + scale) + shift


def _bidx(bm):
    return (lambda b: b) if bm > 1 else (lambda b: 0)


def _ada_kernel(c_ref, w_ref, b_ref, o_ref):
    c = c_ref[...]
    sc = c * jax.nn.sigmoid(c)
    o_ref[0] = jnp.dot(sc, w_ref[0], preferred_element_type=F32,
                       precision=lax.Precision.HIGHEST) + b_ref[0]


def _ada(cc, w_ada, b_ada):
    depth, d, n6 = w_ada.shape
    r = cc.shape[0]
    tn = 1536
    return pl.pallas_call(
        _ada_kernel,
        grid=(depth, n6 // tn),
        in_specs=[pl.BlockSpec((r, d), lambda i, j: (0, 0)),
                  pl.BlockSpec((1, d, tn), lambda i, j: (i, 0, j)),
                  pl.BlockSpec((1, 1, tn), lambda i, j: (i, 0, j))],
        out_specs=pl.BlockSpec((1, r, tn), lambda i, j: (i, 0, j)),
        out_shape=jax.ShapeDtypeStruct((depth, r, n6), F32),
        compiler_params=_cparams(("parallel", "parallel")),
        name="ada",
    )(cc, w_ada, b_ada.reshape(depth, 1, n6))


def _nmm_kernel(*refs, epilogue, n_extra):
    x_ref, g_ref, sh_ref, sc_ref, w_ref = refs[:5]
    extra = refs[5:5 + n_extra]
    outs = refs[5 + n_extra:-1]
    u_ref = refs[-1]
    j = pl.program_id(1)
    n = x_ref.shape[1]

    @pl.when(j == 0)
    def _():
        for r0 in range(0, n, ROW_CHUNK):
            r1 = min(n, r0 + ROW_CHUNK)
            u_ref[r0:r1, :] = _norm_mod(x_ref[0, r0:r1, :], g_ref[...], sh_ref[0], sc_ref[0]).astype(BF16)

    acc = jnp.dot(u_ref[...], w_ref[...], preferred_element_type=F32)
    epilogue(acc, j, extra, outs)


def _nmm(xs, g, shift, scale, w, tn, epilogue, extra, extra_specs, out_shapes, out_specs, name):
    b, n, d = xs.shape
    bi = _bidx(shift.shape[0])
    nout = w.shape[1]
    kern = functools.partial(_nmm_kernel, epilogue=epilogue, n_extra=len(extra))
    return pl.pallas_call(
        kern,
        grid=(b, nout // tn),
        in_specs=[pl.BlockSpec((1, n, d), lambda i, j: (i, 0, 0)),
                  pl.BlockSpec((1, d), lambda i, j: (0, 0)),
                  pl.BlockSpec((1, 1, d), lambda i, j: (bi(i), 0, 0)),
                  pl.BlockSpec((1, 1, d), lambda i, j: (bi(i), 0, 0)),
                  pl.BlockSpec((d, tn), lambda i, j: (0, j))] + list(extra_specs),
        out_specs=out_specs,
        out_shape=out_shapes,
        scratch_shapes=[pltpu.VMEM((n, d), BF16)],
        compiler_params=_cparams(("parallel", "arbitrary")),
        name=name,
    )(xs, g, shift, scale, w, *extra)


def _qkv_epilogue(acc, j, extra, outs, *, use_rope, n_sec, dqk):
    gain_ref, cos_ref, sin_ref, gmat_ref = extra
    o_ref = outs[0]
    tn = acc.shape[1]

    @pl.when(j < 2 * n_sec)
    def _():
        for hb in range(tn // LANES):
            a = acc[:, hb * LANES:(hb + 1) * LANES]
            ss = jnp.dot((a * a).astype(BF16), gmat_ref[...], preferred_element_type=F32)
            y = a * lax.rsqrt(ss * (1.0 / dqk) + EPS) * gain_ref[0]
            if use_rope:
                lane = lax.broadcasted_iota(jnp.int32, y.shape, 1)
                half = dqk // 2
                partner = jnp.where(lane % dqk < half,
                                    pltpu.roll(y, LANES - half, axis=1),
                                    pltpu.roll(y, half, axis=1))
                y = y * cos_ref[...] + partner * sin_ref[...]
            o_ref[0, :, hb * LANES:(hb + 1) * LANES] = y.astype(BF16)

    @pl.when(j >= 2 * n_sec)
    def _():
        o_ref[0] = acc.astype(BF16)


def _qkv_proj(xs, g, shift, scale, w, gains, cos, sin, gmat, use_rope, dqk, name):
    b, n, d = xs.shape
    tn = 256
    n_sec = d // tn
    epi = functools.partial(_qkv_epilogue, use_rope=use_rope, n_sec=n_sec, dqk=dqk)
    extra = (gains, cos, sin, gmat)
    extra_specs = [pl.BlockSpec((1, 1, LANES), lambda i, j: (jnp.minimum(j // n_sec, 1), 0, 0)),
                   pl.BlockSpec((n, LANES), lambda i, j: (0, 0)),
                   pl.BlockSpec((n, LANES), lambda i, j: (0, 0)),
                   pl.BlockSpec((LANES, LANES), lambda i, j: (0, 0))]
    return _nmm(xs, g, shift, scale, w, tn, epi, extra, extra_specs,
                jax.ShapeDtypeStruct((b, n, 3 * d), BF16),
                pl.BlockSpec((1, n, tn), lambda i, j: (i, 0, j)), name)


def _attn_kernel(lam_ref, q_ref, *refs, n_parts, dqk, post_scale):
    k_refs = refs[:n_parts]
    v_refs = refs[n_parts:2 * n_parts]
    sg_ref = refs[2 * n_parts]
    o_ref = refs[2 * n_parts + 1]
    lam = lam_ref[0]
    q = q_ref[0]
    lane = lax.broadcasted_iota(jnp.int32, q.shape, 1)
    zero = jnp.zeros_like(q)
    qm = (jnp.where(lane < dqk, q, zero), jnp.where(lane >= dqk, q, zero))
    dn = (((1,), (1,)), ((), ()))
    es, inv = [], []
    for m in range(2):
        s = [lax.dot_general(qm[m], k_ref[0], dn, preferred_element_type=F32) for k_ref in k_refs]
        mx = functools.reduce(jnp.maximum, [jnp.max(sp, axis=-1, keepdims=True) for sp in s])
        e = [jnp.exp(sp - mx) for sp in s]
        den = functools.reduce(lambda a, c: a + c, [jnp.sum(ep, axis=-1, keepdims=True) for ep in e])
        es.append(e)
        inv.append(1.0 / den)
    c0 = inv[0]
    c1 = lam * inv[1]
    o = None
    for p in range(n_parts):
        a = (es[0][p] * c0 - es[1][p] * c1).astype(BF16)
        t = jnp.dot(a, v_refs[p][0], preferred_element_type=F32)
        o = t if o is None else o + t
    ms = jnp.mean(o * o, axis=-1, keepdims=True)
    o = o * lax.rsqrt(ms + EPS) * (sg_ref[...] * post_scale)
    o_ref[0] = o.astype(BF16)


def _attention(lam, q_arr, kv_arrs, sub_gain, heads, dqk, post_scale, name):
    b, nq, d3 = q_arr.shape
    d = d3 // 3
    dv = 2 * dqk
    tq = min(nq, 256)
    n_parts = len(kv_arrs)
    kern = functools.partial(_attn_kernel, n_parts=n_parts, dqk=dqk, post_scale=post_scale)
    in_specs = [pl.BlockSpec(memory_space=pltpu.SMEM),
                pl.BlockSpec((1, tq, dv), lambda i, h, t: (i, t, h))]
    in_specs += [pl.BlockSpec((1, a.shape[1], dv), lambda i, h, t: (i, 0, heads + h)) for a in kv_arrs]
    in_specs += [pl.BlockSpec((1, a.shape[1], dv), lambda i, h, t: (i, 0, 2 * heads + h)) for a in kv_arrs]
    in_specs += [pl.BlockSpec((1, dv), lambda i, h, t: (0, 0))]
    return pl.pallas_call(
        kern,
        grid=(b, heads, nq // tq),
        in_specs=in_specs,
        out_specs=pl.BlockSpec((1, tq, dv), lambda i, h, t: (i, t, h)),
        out_shape=jax.ShapeDtypeStruct((b, nq, d), BF16),
        compiler_params=_cparams(("parallel", "parallel", "arbitrary")),
        name=name,
    )(lam, q_arr, *kv_arrs, *kv_arrs, sub_gain)


def _mm_res_kernel(a_ref, w_ref, x_ref, g_ref, o_ref):
    y = jnp.dot(a_ref[0], w_ref[...], preferred_element_type=F32)
    o_ref[0] = x_ref[0] + g_ref[0] * y


def _mm_res(a, w, xs, gate, name):
    b, n, k = a.shape
    d = w.shape[1]
    tm = min(n, 512)
    bi = _bidx(gate.shape[0])
    return pl.pallas_call(
        _mm_res_kernel,
        grid=(b, n // tm),
        in_specs=[pl.BlockSpec((1, tm, k), lambda i, t: (i, t, 0)),
                  pl.BlockSpec((k, d), lambda i, t: (0, 0)),
                  pl.BlockSpec((1, tm, d), lambda i, t: (i, t, 0)),
                  pl.BlockSpec((1, 1, d), lambda i, t: (bi(i), 0, 0))],
        out_specs=pl.BlockSpec((1, tm, d), lambda i, t: (i, t, 0)),
        out_shape=jax.ShapeDtypeStruct((b, n, d), F32),
        compiler_params=_cparams(("parallel", "parallel")),
        name=name,
    )(a, w, xs, gate)


def _gelu_epilogue(acc, j, extra, outs):
    outs[0][0] = jax.nn.gelu(acc).astype(BF16)


def _shift_rows(x, s):
    n = x.shape[0]
    row = lax.broadcasted_iota(jnp.int32, x.shape, 0)
    r = pltpu.roll(x, s % n, axis=0)
    if s >= 0:
        return jnp.where(row >= s, r, 0.0)
    return jnp.where(row < n + s, r, 0.0)


def _conv_epilogue(acc, j, extra, outs):
    cw_ref, cb_ref = extra
    width = cw_ref.shape[0]
    left = width // 2
    y = cb_ref[...] + jnp.zeros_like(acc)
    for k in range(width):
        y = y + cw_ref[k:k + 1, :] * _shift_rows(acc, left - k)
    outs[0][0] = y


def _lru_proj(xs, g, shift, scale, w_gate, w_xr, conv_w, conv_b, name):
    b, n, d = xs.shape
    wd = w_gate.shape[1]
    tn = 256
    gg = _nmm(xs, g, shift, scale, w_gate, tn, _gelu_epilogue, (), [],
              jax.ShapeDtypeStruct((b, n, wd), BF16),
              pl.BlockSpec((1, n, tn), lambda i, j: (i, 0, j)), name + "_gate")
    xc = _nmm(xs, g, shift, scale, w_xr, tn, _conv_epilogue, (conv_w, conv_b),
              [pl.BlockSpec((conv_w.shape[0], tn), lambda i, j: (0, j)),
               pl.BlockSpec((1, tn), lambda i, j: (0, j))],
              jax.ShapeDtypeStruct((b, n, wd), F32),
              pl.BlockSpec((1, n, tn), lambda i, j: (i, 0, j)), name + "_conv")
    return gg, xc


def _lru_scan_kernel(*refs, reverse, reset, final, tt):
    if final:
        xc_ref, wab_ref, bias_ref, lam_ref, h0_ref, prev_ref, gg_ref, o_ref, hfin_ref, a_s, b_s, h_s, st = refs
    else:
        xc_ref, wab_ref, bias_ref, lam_ref, h0_ref, o_ref, hfin_ref, a_s, b_s, h_s, st = refs
    i = pl.program_id(0)
    nb = xc_ref.shape[0]
    ng = wab_ref.shape[0]
    rows = nb * tt

    @pl.when(i == 0)
    def _():
        st[...] = h0_ref[...]

    first_t = tt - 1 if reverse else 0
    row = lax.broadcasted_iota(jnp.int32, (rows, LANES), 0)
    for gi in range(ng):
        xg = xc_ref[:, :, gi * LANES:(gi + 1) * LANES].reshape(rows, LANES)
        pre = jnp.dot(xg.astype(BF16), wab_ref[gi], preferred_element_type=F32) + bias_ref[gi]
        r = jax.nn.sigmoid(pre[:, :LANES])
        ig = jax.nn.sigmoid(pre[:, LANES:])
        sp = jax.nn.softplus(-lam_ref[gi])
        log_a = (-LRU_C) * r * sp
        a = jnp.exp(log_a)
        mult = jnp.sqrt(1.0 - a * a)
        if reset:
            mult = jnp.where(jnp.logical_and(row % tt == first_t, i == 0), 1.0, mult)
        a_s[gi] = a
        b_s[gi] = mult * ig * xg

    def step(s, hs):
        t = (tt - 1 - s) if reverse else s
        out = []
        for gi in range(ng):
            a = a_s[gi, pl.ds(t, nb, stride=tt), :]
            bb = b_s[gi, pl.ds(t, nb, stride=tt), :]
            h = a * hs[gi] + bb
            h_s[gi, pl.ds(t, nb, stride=tt), :] = h
            out.append(h)
        return tuple(out)

    hs = lax.fori_loop(0, tt, step, tuple(st[gi] for gi in range(ng)))
    for gi in range(ng):
        st[gi] = hs[gi]
        hfin_ref[gi] = hs[gi]
        hseq = h_s[gi].reshape(nb, tt, LANES)
        sl = slice(gi * LANES, (gi + 1) * LANES)
        if final:
            o_ref[:, :, sl] = ((prev_ref[:, :, sl] + hseq) * gg_ref[:, :, sl].astype(F32)).astype(BF16)
        else:
            o_ref[:, :, sl] = hseq


def _lru_scan(xc, wab, bias, lam, h0, prev, gg, reverse, reset, name):
    b, n, wd = xc.shape
    ng = wab.shape[0]
    tt = min(n, 32)
    nt = n // tt
    final = prev is not None
    tmap = (lambda i: (0, nt - 1 - i, 0)) if reverse else (lambda i: (0, i, 0))
    full3 = lambda i: (0, 0, 0)
    seq_spec = pl.BlockSpec((b, tt, wd), tmap)
    in_specs = [seq_spec,
                pl.BlockSpec(wab.shape, full3),
                pl.BlockSpec(bias.shape, full3),
                pl.BlockSpec(lam.shape, full3),
                pl.BlockSpec(h0.shape, full3)]
    args = [xc, wab, bias, lam, h0]
    if final:
        in_specs += [seq_spec, seq_spec]
        args += [prev, gg]
    kern = functools.partial(_lru_scan_kernel, reverse=reverse, reset=reset, final=final, tt=tt)
    return pl.pallas_call(
        kern,
        grid=(nt,),
        in_specs=in_specs,
        out_specs=[seq_spec, pl.BlockSpec(h0.shape, full3)],
        out_shape=[jax.ShapeDtypeStruct((b, n, wd), BF16 if final else F32),
                   jax.ShapeDtypeStruct(h0.shape, F32)],
        scratch_shapes=[pltpu.VMEM((ng, b * tt, LANES), F32),
                        pltpu.VMEM((ng, b * tt, LANES), F32),
                        pltpu.VMEM((ng, b * tt, LANES), F32),
                        pltpu.VMEM(h0.shape, F32)],
        compiler_params=_cparams(("arbitrary",)),
        name=name,
    )(*args)


def _window_mean_minus_self(x, window):
    n = x.shape[0]
    back, fwd = x, x
    span = 1
    while span < window // 2:
        back = back + _shift_rows(back, span)
        fwd = fwd + _shift_rows(fwd, -span)
        span *= 2
    p = _shift_rows(back, 1) + fwd
    t = lax.broadcasted_iota(jnp.int32, x.shape, 0)
    lo = jnp.maximum(t - window // 2, 0)
    hi = jnp.minimum(t + window // 2, n)
    cnt = (hi - lo).astype(F32)
    return p / cnt - x


def _pool_epilogue(acc, j, extra, outs):
    wg_ref, ps_ref, xcol_ref, gate_ref = extra
    o_ref = outs[0]
    for gidx, window in enumerate(POOL_WINDOWS):
        @pl.when(j == gidx)
        def _(window=window):
            pooled = _window_mean_minus_self(acc, window).astype(BF16)
            y = jnp.dot(pooled, wg_ref[0], preferred_element_type=F32) * ps_ref[...]
            o_ref[0] = xcol_ref[0] + gate_ref[0] * y


def _pool_mixer(xs, g, shift, scale, w_in, w_grp, pscale, gate, name):
    b, n, d = xs.shape
    ngrp, gd, _ = w_grp.shape
    bi = _bidx(gate.shape[0])
    extra = (w_grp, pscale, xs, gate)
    extra_specs = [pl.BlockSpec((1, gd, gd), lambda i, j: (j, 0, 0)),
                   pl.BlockSpec((1, gd), lambda i, j: (0, j)),
                   pl.BlockSpec((1, n, gd), lambda i, j: (i, 0, j)),
                   pl.BlockSpec((1, 1, gd), lambda i, j: (bi(i), 0, j))]
    return _nmm(xs, g, shift, scale, w_in, gd, _pool_epilogue, extra, extra_specs,
                jax.ShapeDtypeStruct((b, n, d), F32),
                pl.BlockSpec((1, n, gd), lambda i, j: (i, 0, j)), name)


def _route_kernel(x_ref, g_ref, sh_ref, sc_ref, rhi_ref, rlo_ref, u_ref, meta_ref, cnt_ref, *, n_groups):
    u = _norm_mod(x_ref[0], g_ref[...], sh_ref[0], sc_ref[0])
    tm = u.shape[0]
    n_chunks = u.shape[1] // LANES
    for jc in range(n_chunks):
        u_ref[pl.ds(jc, tm, stride=n_chunks), :] = u[:, jc * LANES:(jc + 1) * LANES]
    u_hi = u.astype(BF16)
    u_lo = (u - u_hi.astype(F32)).astype(BF16)
    lg = (jnp.dot(u_hi, rhi_ref[...], preferred_element_type=F32)
          + jnp.dot(u_lo, rhi_ref[...], preferred_element_type=F32)
          + jnp.dot(u_hi, rlo_ref[...], preferred_element_type=F32))
    lane = lax.broadcasted_iota(jnp.int32, lg.shape, 1)
    neg = jnp.float32(-jnp.inf)
    big = jnp.int32(1 << 20)
    is_g = lane < n_groups
    gmax = jnp.max(jnp.where(is_g, lg, neg), axis=-1, keepdims=True)
    g_sel = jnp.min(jnp.where(jnp.logical_and(is_g, lg == gmax), lane, big), axis=-1, keepdims=True)
    p_sel = 1.0 / jnp.sum(jnp.where(is_g, jnp.exp(lg - gmax), 0.0), axis=-1, keepdims=True)
    e_idx = lane - n_groups
    in_grp = jnp.logical_and(e_idx >= g_sel * EXPERTS_PER_GROUP, e_idx < (g_sel + 1) * EXPERTS_PER_GROUP)
    m1 = jnp.max(jnp.where(in_grp, lg, neg), axis=-1, keepdims=True)
    i1 = jnp.min(jnp.where(jnp.logical_and(in_grp, lg == m1), lane, big), axis=-1, keepdims=True)
    rest = jnp.logical_and(in_grp, lane != i1)
    m2 = jnp.max(jnp.where(rest, lg, neg), axis=-1, keepdims=True)
    i2 = jnp.min(jnp.where(jnp.logical_and(rest, lg == m2), lane, big), axis=-1, keepdims=True)
    e2 = jnp.exp(m2 - m1)
    w1 = p_sel / (1.0 + e2)
    w2 = w1 * e2
    off = n_groups + g_sel * EXPERTS_PER_GROUP
    la = jnp.minimum(i1, i2) - off
    lb = jnp.maximum(i1, i2) - off
    pair = ((la * (2 * EXPERTS_PER_GROUP - 1 - la)) >> 1) + lb - la - 1
    cls = g_sel * PAIRS_PER_GROUP + pair
    c_lo = jnp.where(i1 < i2, w1, w2)
    c_hi = jnp.where(i1 < i2, w2, w1)
    onehot = lane == cls
    row = lax.broadcasted_iota(jnp.int32, (tm, tm), 0)
    col = lax.broadcasted_iota(jnp.int32, (tm, tm), 1)
    tri = jnp.where(row > col, 1.0, 0.0).astype(BF16)
    before = jnp.dot(tri, jnp.where(onehot, 1.0, 0.0).astype(BF16), preferred_element_type=F32)
    rank = jnp.sum(jnp.where(onehot, before, 0.0), axis=-1, keepdims=True)
    meta_ref[0] = jnp.where(lane == 0, cls.astype(F32),
                            jnp.where(lane == 1, rank,
                                      jnp.where(lane == 2, c_lo, jnp.where(lane == 3, c_hi, 0.0))))
    cnt = jnp.sum(jnp.where(onehot, 1.0, 0.0), axis=0, keepdims=True)
    cnt_ref[0] = jnp.broadcast_to(cnt, cnt_ref.shape[1:])


def _route(xs, g, shift, scale, r_hi, r_lo, n_groups, name):
    b, n, d = xs.shape
    tm = min(n, ROUTE_BLOCK)
    n_chunks = d // LANES
    nt = n // tm
    bi = _bidx(shift.shape[0])
    kern = functools.partial(_route_kernel, n_groups=n_groups)
    return pl.pallas_call(
        kern,
        grid=(b, nt),
        in_specs=[pl.BlockSpec((1, tm, d), lambda i, t: (i, t, 0)),
                  pl.BlockSpec((1, d), lambda i, t: (0, 0)),
                  pl.BlockSpec((1, 1, d), lambda i, t: (bi(i), 0, 0)),
                  pl.BlockSpec((1, 1, d), lambda i, t: (bi(i), 0, 0)),
                  pl.BlockSpec((d, LANES), lambda i, t: (0, 0)),
                  pl.BlockSpec((d, LANES), lambda i, t: (0, 0))],
        out_specs=[pl.BlockSpec((tm * n_chunks, LANES), lambda i, t: (i * nt + t, 0)),
                   pl.BlockSpec((1, tm, LANES), lambda i, t: (i, t, 0)),
                   pl.BlockSpec((1, 8, LANES), lambda i, t: (i * nt + t, 0, 0))],
        out_shape=[jax.ShapeDtypeStruct((b * n * n_chunks, LANES), F32),
                   jax.ShapeDtypeStruct((b, n, LANES), F32),
                   jax.ShapeDtypeStruct((b * nt, 8, LANES), F32)],
        compiler_params=_cparams(("parallel", "parallel")),
        name=name,
    )(xs, g, shift, scale, r_hi, r_lo)


def _ring_dma(make, n):
    ring = min(DMA_RING, n)

    def start(k, c):
        make(k).start()
        return c

    def turn(k, c):
        make(k - ring).wait()
        make(k).start()
        return c

    def drain(k, c):
        make(k).wait()
        return c

    lax.fori_loop(0, ring, start, 0)
    lax.fori_loop(ring, n, turn, 0)
    lax.fori_loop(n - ring, n, drain, 0)


def _dispatch_kernel(pos_ref, u_ref, buf_in_ref, buf_ref, sems):
    del buf_in_ref
    tb = pos_ref.shape[-1]
    t0 = pl.program_id(0) * tb

    def make(k):
        return pltpu.make_async_copy(u_ref.at[t0 + k], buf_ref.at[pos_ref[0, 0, k]],
                                     sems.at[k % DMA_RING])

    _ring_dma(make, tb)


def _dispatch(pos, u3, buf, name):
    nb, _, tb = pos.shape
    return pl.pallas_call(
        _dispatch_kernel,
        grid=(nb,),
        in_specs=[pl.BlockSpec((1, 1, tb), lambda i: (i, 0, 0), memory_space=pltpu.SMEM),
                  pl.BlockSpec(memory_space=pl.ANY),
                  pl.BlockSpec(memory_space=pl.ANY)],
        out_specs=pl.BlockSpec(memory_space=pl.ANY),
        out_shape=jax.ShapeDtypeStruct(buf.shape, buf.dtype),
        scratch_shapes=[pltpu.SemaphoreType.DMA((DMA_RING,))],
        input_output_aliases={2: 0},
        compiler_params=pltpu.CompilerParams(dimension_semantics=("arbitrary",), has_side_effects=True),
        name=name,
    )(pos, u3, buf)


def _experts_kernel(ea_ref, eb_ref, nused_ref, x_ref, w1a_ref, w3a_ref, w2a_ref, w1b_ref, w3b_ref, w2b_ref,
                    y_ref):
    del ea_ref, eb_ref
    k = pl.program_id(0)
    n_chunks = w1a_ref.shape[1] // LANES
    mt = x_ref.shape[0] // n_chunks

    @pl.when(k < nused_ref[0])
    def _():
        x = jnp.concatenate([x_ref[pl.ds(jc, mt, stride=n_chunks), :] for jc in range(n_chunks)],
                            axis=-1).astype(BF16)
        for half, (w1_ref, w3_ref, w2_ref) in enumerate(((w1a_ref, w3a_ref, w2a_ref),
                                                         (w1b_ref, w3b_ref, w2b_ref))):
            h1 = jnp.dot(x, w1_ref[0], preferred_element_type=F32)
            h3 = jnp.dot(x, w3_ref[0], preferred_element_type=F32)
            he = (h1 * jax.nn.sigmoid(h1) * h3).astype(BF16)
            y = jnp.dot(he, w2_ref[0], preferred_element_type=F32)
            for jc in range(n_chunks):
                y_ref[pl.ds(half * n_chunks + jc, mt, stride=2 * n_chunks), :] = y[:, jc * LANES:(jc + 1) * LANES]

    @pl.when(k >= nused_ref[0])
    def _():
        y_ref[...] = jnp.zeros_like(y_ref)


def _experts(tile_ea, tile_eb, n_used, xbuf, w1, w3, w2, mt, name):
    ne, d, f = w1.shape
    n_chunks = d // LANES
    n_tiles = xbuf.shape[0] // (mt * n_chunks)
    wa = lambda k, ea, eb, nu: (ea[k], 0, 0)
    wb = lambda k, ea, eb, nu: (eb[k], 0, 0)
    grid_spec = pltpu.PrefetchScalarGridSpec(
        num_scalar_prefetch=3,
        grid=(n_tiles,),
        in_specs=[pl.BlockSpec((mt * n_chunks, LANES), lambda k, ea, eb, nu: (k, 0)),
                  pl.BlockSpec((1, d, f), wa), pl.BlockSpec((1, d, f), wa), pl.BlockSpec((1, f, d), wa),
                  pl.BlockSpec((1, d, f), wb), pl.BlockSpec((1, d, f), wb), pl.BlockSpec((1, f, d), wb)],
        out_specs=pl.BlockSpec((mt * 2 * n_chunks, LANES), lambda k, ea, eb, nu: (k, 0)),
    )
    return pl.pallas_call(
        _experts_kernel,
        grid_spec=grid_spec,
        out_shape=jax.ShapeDtypeStruct((n_tiles * mt * 2 * n_chunks, LANES), F32),
        compiler_params=_cparams(("arbitrary",)),
        name=name,
    )(tile_ea, tile_eb, n_used, xbuf, w1, w3, w2, w1, w3, w2)


def _combine_kernel(pos_ref, y_hbm, meta_ref, x_ref, g_ref, o_ref, buf, sems):
    tb = pos_ref.shape[-1]
    rows = y_hbm.shape[1]
    n_chunks = rows // 2

    def make(k):
        dst = buf.at[pl.ds(pl.multiple_of(k * rows, rows), rows)]
        return pltpu.make_async_copy(y_hbm.at[pos_ref[0, 0, k]], dst, sems.at[k % DMA_RING])

    _ring_dma(make, tb)
    meta = meta_ref[0]
    c_lo = meta[:, 2:3]
    c_hi = meta[:, 3:4]
    for jc in range(n_chunks):
        sl = slice(jc * LANES, (jc + 1) * LANES)
        y_lo = buf[pl.ds(jc, tb, stride=rows), :]
        y_hi = buf[pl.ds(n_chunks + jc, tb, stride=rows), :]
        o_ref[0, :, sl] = x_ref[0, :, sl] + g_ref[0, :, sl] * (c_lo * y_lo + c_hi * y_hi)


def _combine(pos, y3, meta, xs, gate, name):
    b, n, d = xs.shape
    tb = pos.shape[-1]
    nt = n // tb
    rows = y3.shape[1]
    bi = _bidx(gate.shape[0])
    return pl.pallas_call(
        _combine_kernel,
        grid=(b, nt),
        in_specs=[pl.BlockSpec((1, 1, tb), lambda i, t: (i * nt + t, 0, 0), memory_space=pltpu.SMEM),
                  pl.BlockSpec(memory_space=pl.ANY),
                  pl.BlockSpec((1, tb, LANES), lambda i, t: (i, t, 0)),
                  pl.BlockSpec((1, tb, d), lambda i, t: (i, t, 0)),
                  pl.BlockSpec((1, 1, d), lambda i, t: (bi(i), 0, 0))],
        out_specs=pl.BlockSpec((1, tb, d), lambda i, t: (i, t, 0)),
        out_shape=jax.ShapeDtypeStruct((b, n, d), F32),
        scratch_shapes=[pltpu.VMEM((tb * rows, LANES), F32),
                        pltpu.SemaphoreType.DMA((DMA_RING,))],
        compiler_params=_cparams(("arbitrary", "arbitrary")),
        name=name,
    )(pos, y3, meta, xs, gate)


def _hier_moe(streams, g, r_hi, r_lo, w1, w3, w2, n_groups, name):
    d = w1.shape[1]
    n_chunks = d // LANES
    n_cls = n_groups * PAIRS_PER_GROUP
    routed = [_route(xs, g, sh, sc, r_hi, r_lo, n_groups, f"{name}_route{si}")
              for si, (xs, sh, sc, _) in enumerate(streams)]
    cnt = jnp.concatenate([r[2][:, 0, :n_cls] for r in routed], axis=0).astype(jnp.int32)
    total = jnp.sum(cnt, axis=0)
    padded = ((total + MOE_TILE - 1) // MOE_TILE) * MOE_TILE
    cls_end = jnp.cumsum(padded)
    base = (cls_end - padded)[None, :] + jnp.cumsum(cnt, axis=0) - cnt
    n_tok = sum(xs.shape[0] * xs.shape[1] for xs, _, _, _ in streams)
    n_tiles = n_tok // MOE_TILE + n_cls
    tile_start = jnp.arange(n_tiles, dtype=jnp.int32) * MOE_TILE
    n_used = (cls_end[-1] // MOE_TILE).astype(jnp.int32)
    tile_cls = jnp.minimum(jnp.searchsorted(cls_end, tile_start, side="right"), n_cls - 1)
    tile_cls = jnp.where(tile_start < cls_end[-1], tile_cls, tile_cls[jnp.maximum(n_used - 1, 0)])
    pairs = [(a, b) for a in range(EXPERTS_PER_GROUP) for b in range(a + 1, EXPERTS_PER_GROUP)]
    lo_tab = jnp.array([gi * EXPERTS_PER_GROUP + a for gi in range(n_groups) for a, _ in pairs], jnp.int32)
    hi_tab = jnp.array([gi * EXPERTS_PER_GROUP + b for gi in range(n_groups) for _, b in pairs], jnp.int32)
    tile_ea = lo_tab[tile_cls]
    tile_eb = hi_tab[tile_cls]

    buf = jnp.zeros((n_tiles * MOE_TILE, n_chunks, LANES), F32)
    poss, blk0 = [], 0
    for si, ((xs, _, _, _), (u3, meta, c)) in enumerate(zip(streams, routed)):
        nb = c.shape[0]
        tm = xs.shape[1] * xs.shape[0] // nb
        cls_t = meta[..., 0].astype(jnp.int32).reshape(nb, tm)
        rank_t = meta[..., 1].astype(jnp.int32).reshape(nb, tm)
        pos = jnp.take_along_axis(base[blk0:blk0 + nb], cls_t, axis=1) + rank_t
        pos = pos.reshape(nb, 1, tm)
        poss.append(pos)
        buf = _dispatch(pos, u3.reshape(-1, n_chunks, LANES), buf, f"{name}_dispatch{si}")
        blk0 += nb
    y = _experts(tile_ea, tile_eb, n_used.reshape(1), buf.reshape(-1, LANES), w1, w3, w2, MOE_TILE,
                 f"{name}_experts")
    y3 = y.reshape(-1, 2 * n_chunks, LANES)
    outs = []
    for si, ((xs, _, _, gate), (_, meta, _), pos) in enumerate(zip(streams, routed, poss)):
        outs.append(_combine(pos, y3, meta, xs, gate, f"{name}_combine{si}"))
    return outs


def _rope_tables(n, dqk):
    rows = n // GRID_W
    row = jnp.repeat(jnp.arange(rows), GRID_W).astype(F32)
    col = jnp.tile(jnp.arange(GRID_W), rows).astype(F32)
    n_freq = dqk // 4
    inv = ROPE_BASE ** (-jnp.arange(n_freq, dtype=F32) / n_freq)
    ang = jnp.concatenate([row[:, None] * inv, col[:, None] * inv], axis=-1)
    cos, sin = jnp.cos(ang), jnp.sin(ang)
    reps = LANES // dqk
    cos_l = jnp.tile(jnp.concatenate([cos, cos], axis=-1), (1, reps))
    sin_l = jnp.tile(jnp.concatenate([-sin, sin], axis=-1), (1, reps))
    return cos_l, sin_l


def kernel(x, c, ctx, c_ctx, w_ada, b_ada, norm_g, attn_w_in, attn_q_gain, attn_k_gain, attn_lam, attn_sub_gain, attn_w_out, lru_w_in, lru_conv_w, lru_conv_b, lru_w_a, lru_b_a, lru_w_x, lru_b_x, lru_lam, lru_w_out, pool_w_in, pool_w_grp, pool_scale, moe_router_g, moe_router_e, moe_w1, moe_w3, moe_w2):
    bsz, seq, d = x.shape
    n_ctx = ctx.shape[1]
    depth = w_ada.shape[0]
    dqk = attn_q_gain.shape[-1]
    heads = d // (2 * dqk)
    n_groups = moe_router_g.shape[-1]
    n_experts = moe_router_e.shape[-1]
    lru_w = lru_w_out.shape[1]
    lru_g = lru_w_a.shape[2]

    pad = (-(bsz + 1)) % 8
    cc = jnp.concatenate([c, c_ctx[None, :], jnp.zeros((pad, d), F32)], axis=0)
    mods = _ada(cc, w_ada, b_ada)

    cos_l, sin_l = _rope_tables(seq, dqk)
    cos_c = jnp.ones((n_ctx, LANES), F32)
    sin_c = jnp.zeros((n_ctx, LANES), F32)
    lane = jnp.arange(LANES)
    gmat = (lane[:, None] // dqk == lane[None, :] // dqk).astype(BF16)

    xs, cs = x, ctx
    for i in range(depth):
        need_ctx = i < depth - 1
        mx = [mods[i, :bsz, k * d:(k + 1) * d][:, None, :] for k in range(6)]
        mc = [mods[i, bsz:bsz + 1, k * d:(k + 1) * d][:, None, :] for k in range(6)]
        g1 = norm_g[i, 0][None, :]
        g2 = norm_g[i, 1][None, :]
        kind, j = i % 3, i // 3
        nm = f"l{i}"
        if kind == 0:
            lam_init = 0.8 - 0.6 * math.exp(-0.3 * i)
            lq = attn_lam[j].astype(F32)
            lam = (jnp.exp(jnp.sum(lq[0] * lq[1])) - jnp.exp(jnp.sum(lq[2] * lq[3])) + lam_init).reshape(1)
            w_in = attn_w_in[j].astype(BF16)
            reps = LANES // dqk
            gains = jnp.stack([jnp.tile(attn_q_gain[j], reps) * (dqk ** -0.5),
                               jnp.tile(attn_k_gain[j], reps)])[:, None, :]
            sub_gain = attn_sub_gain[j][None, :]
            qkv_x = _qkv_proj(xs, g1, mx[0], mx[1], w_in, gains, cos_l, sin_l, gmat, True, dqk, nm + "_qkv_x")
            qkv_c = _qkv_proj(cs, g1, mc[0], mc[1], w_in, gains, cos_c, sin_c, gmat, False, dqk, nm + "_qkv_c")
            w_out = attn_w_out[j].astype(BF16)
            ox = _attention(lam, qkv_x, [qkv_c, qkv_x], sub_gain, heads, dqk, 1.0 - lam_init, nm + "_attn_x")
            xs = _mm_res(ox, w_out, xs, mx[2], nm + "_out_x")
            if need_ctx:
                oc = _attention(lam, qkv_c, [qkv_c], sub_gain, heads, dqk, 1.0 - lam_init, nm + "_attn_c")
                cs = _mm_res(oc, w_out, cs, mc[2], nm + "_out_c")
        elif kind == 1:
            w_in = lru_w_in[j].astype(BF16)
            w_gate, w_xr = w_in[:, :lru_w], w_in[:, lru_w:]
            conv_b = lru_conv_b[j][None, :]
            ggx, xcx = _lru_proj(xs, g1, mx[0], mx[1], w_gate, w_xr, lru_conv_w[j], conv_b, nm + "_proj_x")
            ggc, xcc = _lru_proj(cs, g1, mc[0], mc[1], w_gate, w_xr, lru_conv_w[j], conv_b, nm + "_proj_c")
            h0 = jnp.zeros((lru_g, bsz, LANES), F32)
            hx_prev, hc_prev = None, None
            for dr, reverse in enumerate((False, True)):
                wab = jnp.concatenate([lru_w_a[j, dr], lru_w_x[j, dr]], axis=-1).astype(BF16)
                bias = jnp.concatenate([lru_b_a[j, dr].reshape(lru_g, 1, LANES),
                                        lru_b_x[j, dr].reshape(lru_g, 1, LANES)], axis=-1)
                lam_d = lru_lam[j, dr].reshape(lru_g, 1, LANES)
                last = dr == 1
                hc_seq, hc_fin = _lru_scan(xcc, wab, bias, lam_d, h0, hc_prev if last else None,
                                           ggc if last else None, reverse, True, f"{nm}_scan_c{dr}")
                hx_seq, _ = _lru_scan(xcx, wab, bias, lam_d, hc_fin, hx_prev if last else None,
                                      ggx if last else None, reverse, False, f"{nm}_scan_x{dr}")
                hx_prev, hc_prev = hx_seq, hc_seq
            w_out = lru_w_out[j].astype(BF16)
            xs = _mm_res(hx_prev, w_out, xs, mx[2], nm + "_out_x")
            if need_ctx:
                cs = _mm_res(hc_prev, w_out, cs, mc[2], nm + "_out_c")
        else:
            w_in = pool_w_in[j].astype(BF16)
            w_grp = pool_w_grp[j].astype(BF16)
            pscale = pool_scale[j][None, :]
            xs_new = _pool_mixer(xs, g1, mx[0], mx[1], w_in, w_grp, pscale, mx[2], nm + "_pool_x")
            if need_ctx:
                cs = _pool_mixer(cs, g1, mc[0], mc[1], w_in, w_grp, pscale, mc[2], nm + "_pool_c")
            xs = xs_new

        r_all = jnp.concatenate([moe_router_g[i], moe_router_e[i],
                                 jnp.zeros((d, LANES - n_groups - n_experts), F32)], axis=-1)
        r_hi = r_all.astype(BF16)
        r_lo = (r_all - r_hi.astype(F32)).astype(BF16)
        w1 = moe_w1[i].astype(BF16)
        w3 = moe_w3[i].astype(BF16)
        w2 = moe_w2[i].astype(BF16)
        streams = [(xs, mx[3], mx[4], mx[5])]
        if need_ctx:
            streams.append((cs, mc[3], mc[4], mc[5]))
        outs = _hier_moe(streams, g2, r_hi, r_lo, w1, w3, w2, n_groups, nm + "_moe")
        xs = outs[0]
        if need_ctx:
            cs = outs[1]
    return xs
```

```python
import functools
import math

import jax
import jax.numpy as jnp
from jax import lax
from jax.experimental import pallas as pl
from jax.experimental.pallas import tpu as pltpu

F32 = jnp.float32
BF16 = jnp.bfloat16

EPS = 1e-6
GRID_W = 64
ROPE_BASE = 10000.0
LRU_C = 8.0
POOL_WINDOWS = (2, 4, 8, 16)
EXPERTS_PER_GROUP = 4
PAIRS_PER_GROUP = EXPERTS_PER_GROUP * (EXPERTS_PER_GROUP - 1) // 2
ROUTE_BLOCK = 512
MOE_TILE = 512
DMA_RING = 256
LANES = 128
VMEM_LIMIT = 56 << 20
ROW_CHUNK = 512


def _cparams(sem):
    return pltpu.CompilerParams(dimension_semantics=sem, vmem_limit_bytes=VMEM_LIMIT)


def _norm_mod(x, g, shift, scale):
    ms = jnp.mean(x * x, axis=-1, keepdims=True)
    return (x * lax.rsqrt(ms + EPS) * g) * (1.0 + scale) + shift


def _bidx(bm):
    return (lambda b: b) if bm > 1 else (lambda b: 0)


def _ada_kernel(c_ref, w_ref, b_ref, o_ref):
    c = c_ref[...]
    sc = c * jax.nn.sigmoid(c)
    o_ref[0] = jnp.dot(sc, w_ref[0], preferred_element_type=F32,
                       precision=lax.Precision.HIGHEST) + b_ref[0]


def _ada(cc, w_ada, b_ada):
    depth, d, n6 = w_ada.shape
    r = cc.shape[0]
    tn = 1536
    return pl.pallas_call(
        _ada_kernel,
        grid=(depth, n6 // tn),
        in_specs=[pl.BlockSpec((r, d), lambda i, j: (0, 0)),
                  pl.BlockSpec((1, d, tn), lambda i, j: (i, 0, j)),
                  pl.BlockSpec((1, 1, tn), lambda i, j: (i, 0, j))],
        out_specs=pl.BlockSpec((1, r, tn), lambda i, j: (i, 0, j)),
        out_shape=jax.ShapeDtypeStruct((depth, r, n6), F32),
        compiler_params=_cparams(("parallel", "parallel")),
        name="ada",
    )(cc, w_ada, b_ada.reshape(depth, 1, n6))


def _nmm_kernel(*refs, epilogue, n_extra):
    x_ref, g_ref, sh_ref, sc_ref, w_ref = refs[:5]
    extra = refs[5:5 + n_extra]
    outs = refs[5 + n_extra:-1]
    u_ref = refs[-1]
    j = pl.program_id(1)
    n = x_ref.shape[1]

    @pl.when(j == 0)
    def _():
        for r0 in range(0, n, ROW_CHUNK):
            r1 = min(n, r0 + ROW_CHUNK)
            u_ref[r0:r1, :] = _norm_mod(x_ref[0, r0:r1, :], g_ref[...], sh_ref[0], sc_ref[0]).astype(BF16)

    acc = jnp.dot(u_ref[...], w_ref[...], preferred_element_type=F32)
    epilogue(acc, j, extra, outs)


def _nmm(xs, g, shift, scale, w, tn, epilogue, extra, extra_specs, out_shapes, out_specs, name):
    b, n, d = xs.shape
    bi = _bidx(shift.shape[0])
    nout = w.shape[1]
    kern = functools.partial(_nmm_kernel, epilogue=epilogue, n_extra=len(extra))
    return pl.pallas_call(
        kern,
        grid=(b, nout // tn),
        in_specs=[pl.BlockSpec((1, n, d), lambda i, j: (i, 0, 0)),
                  pl.BlockSpec((1, d), lambda i, j: (0, 0)),
                  pl.BlockSpec((1, 1, d), lambda i, j: (bi(i), 0, 0)),
                  pl.BlockSpec((1, 1, d), lambda i, j: (bi(i), 0, 0)),
                  pl.BlockSpec((d, tn), lambda i, j: (0, j))] + list(extra_specs),
        out_specs=out_specs,
        out_shape=out_shapes,
        scratch_shapes=[pltpu.VMEM((n, d), BF16)],
        compiler_params=_cparams(("parallel", "arbitrary")),
        name=name,
    )(xs, g, shift, scale, w, *extra)


def _qkv_epilogue(acc, j, extra, outs, *, use_rope, n_sec, dqk):
    gain_ref, cos_ref, sin_ref, gmat_ref = extra
    o_ref = outs[0]
    tn = acc.shape[1]

    @pl.when(j < 2 * n_sec)
    def _():
        for hb in range(tn // LANES):
            a = acc[:, hb * LANES:(hb + 1) * LANES]
            ss = jnp.dot((a * a).astype(BF16), gmat_ref[...], preferred_element_type=F32)
            y = a * lax.rsqrt(ss * (1.0 / dqk) + EPS) * gain_ref[0]
            if use_rope:
                lane = lax.broadcasted_iota(jnp.int32, y.shape, 1)
                half = dqk // 2
                partner = jnp.where(lane % dqk < half,
                                    pltpu.roll(y, LANES - half, axis=1),
                                    pltpu.roll(y, half, axis=1))
                y = y * cos_ref[...] + partner * sin_ref[...]
            o_ref[0, :, hb * LANES:(hb + 1) * LANES] = y.astype(BF16)

    @pl.when(j >= 2 * n_sec)
    def _():
        o_ref[0] = acc.astype(BF16)


def _qkv_proj(xs, g, shift, scale, w, gains, cos, sin, gmat, use_rope, dqk, name):
    b, n, d = xs.shape
    tn = 256
    n_sec = d // tn
    epi = functools.partial(_qkv_epilogue, use_rope=use_rope, n_sec=n_sec, dqk=dqk)
    extra = (gains, cos, sin, gmat)
    extra_specs = [pl.BlockSpec((1, 1, LANES), lambda i, j: (jnp.minimum(j // n_sec, 1), 0, 0)),
                   pl.BlockSpec((n, LANES), lambda i, j: (0, 0)),
                   pl.BlockSpec((n, LANES), lambda i, j: (0, 0)),
                   pl.BlockSpec((LANES, LANES), lambda i, j: (0, 0))]
    return _nmm(xs, g, shift, scale, w, tn, epi, extra, extra_specs,
                jax.ShapeDtypeStruct((b, n, 3 * d), BF16),
                pl.BlockSpec((1, n, tn), lambda i, j: (i, 0, j)), name)


def _attn_kernel(lam_ref, q_ref, *refs, n_parts, dqk, post_scale):
    k_refs = refs[:n_parts]
    v_refs = refs[n_parts:2 * n_parts]
    sg_ref = refs[2 * n_parts]
    o_ref = refs[2 * n_parts + 1]
    lam = lam_ref[0]
    q = q_ref[0]
    lane = lax.broadcasted_iota(jnp.int32, q.shape, 1)
    zero = jnp.zeros_like(q)
    qm = (jnp.where(lane < dqk, q, zero), jnp.where(lane >= dqk, q, zero))
    dn = (((1,), (1,)), ((), ()))
    es, inv = [], []
    for m in range(2):
        s = [lax.dot_general(qm[m], k_ref[0], dn, preferred_element_type=F32) for k_ref in k_refs]
        mx = functools.reduce(jnp.maximum, [jnp.max(sp, axis=-1, keepdims=True) for sp in s])
        e = [jnp.exp(sp - mx) for sp in s]
        den = functools.reduce(lambda a, c: a + c, [jnp.sum(ep, axis=-1, keepdims=True) for ep in e])
        es.append(e)
        inv.append(1.0 / den)
    c0 = inv[0]
    c1 = lam * inv[1]
    o = None
    for p in range(n_parts):
        a = (es[0][p] * c0 - es[1][p] * c1).astype(BF16)
        t = jnp.dot(a, v_refs[p][0], preferred_element_type=F32)
        o = t if o is None else o + t
    ms = jnp.mean(o * o, axis=-1, keepdims=True)
    o = o * lax.rsqrt(ms + EPS) * (sg_ref[...] * post_scale)
    o_ref[0] = o.astype(BF16)


def _attention(lam, q_arr, kv_arrs, sub_gain, heads, dqk, post_scale, name):
    b, nq, d3 = q_arr.shape
    d = d3 // 3
    dv = 2 * dqk
    tq = min(nq, 256)
    n_parts = len(kv_arrs)
    kern = functools.partial(_attn_kernel, n_parts=n_parts, dqk=dqk, post_scale=post_scale)
    in_specs = [pl.BlockSpec(memory_space=pltpu.SMEM),
                pl.BlockSpec((1, tq, dv), lambda i, h, t: (i, t, h))]
    in_specs += [pl.BlockSpec((1, a.shape[1], dv), lambda i, h, t: (i, 0, heads + h)) for a in kv_arrs]
    in_specs += [pl.BlockSpec((1, a.shape[1], dv), lambda i, h, t: (i, 0, 2 * heads + h)) for a in kv_arrs]
    in_specs += [pl.BlockSpec((1, dv), lambda i, h, t: (0, 0))]
    return pl.pallas_call(
        kern,
        grid=(b, heads, nq // tq),
        in_specs=in_specs,
        out_specs=pl.BlockSpec((1, tq, dv), lambda i, h, t: (i, t, h)),
        out_shape=jax.ShapeDtypeStruct((b, nq, d), BF16),
        compiler_params=_cparams(("parallel", "parallel", "arbitrary")),
        name=name,
    )(lam, q_arr, *kv_arrs, *kv_arrs, sub_gain)


def _mm_res_kernel(a_ref, w_ref, x_ref, g_ref, o_ref):
    y = jnp.dot(a_ref[0], w_ref[...], preferred_element_type=F32)
    o_ref[0] = x_ref[0] + g_ref[0] * y


def _mm_res(a, w, xs, gate, name):
    b, n, k = a.shape
    d = w.shape[1]
    tm = min(n, 512)
    bi = _bidx(gate.shape[0])
    return pl.pallas_call(
        _mm_res_kernel,
        grid=(b, n // tm),
        in_specs=[pl.BlockSpec((1, tm, k), lambda i, t: (i, t, 0)),
                  pl.BlockSpec((k, d), lambda i, t: (0, 0)),
                  pl.BlockSpec((1, tm, d), lambda i, t: (i, t, 0)),
                  pl.BlockSpec((1, 1, d), lambda i, t: (bi(i), 0, 0))],
        out_specs=pl.BlockSpec((1, tm, d), lambda i, t: (i, t, 0)),
        out_shape=jax.ShapeDtypeStruct((b, n, d), F32),
        compiler_params=_cparams(("parallel", "parallel")),
        name=name,
    )(a, w, xs, gate)


def _gelu_epilogue(acc, j, extra, outs):
    outs[0][0] = jax.nn.gelu(acc).astype(BF16)


def _shift_rows(x, s):
    n = x.shape[0]
    row = lax.broadcasted_iota(jnp.int32, x.shape, 0)
    r = pltpu.roll(x, s % n, axis=0)
    if s >= 0:
        return jnp.where(row >= s, r, 0.0)
    return jnp.where(row < n + s, r, 0.0)


def _conv_epilogue(acc, j, extra, outs):
    cw_ref, cb_ref = extra
    width = cw_ref.shape[0]
    left = width // 2
    y = cb_ref[...] + jnp.zeros_like(acc)
    for k in range(width):
        y = y + cw_ref[k:k + 1, :] * _shift_rows(acc, left - k)
    outs[0][0] = y


def _lru_proj(xs, g, shift, scale, w_gate, w_xr, conv_w, conv_b, name):
    b, n, d = xs.shape
    wd = w_gate.shape[1]
    tn = 256
    gg = _nmm(xs, g, shift, scale, w_gate, tn, _gelu_epilogue, (), [],
              jax.ShapeDtypeStruct((b, n, wd), BF16),
              pl.BlockSpec((1, n, tn), lambda i, j: (i, 0, j)), name + "_gate")
    xc = _nmm(xs, g, shift, scale, w_xr, tn, _conv_epilogue, (conv_w, conv_b),
              [pl.BlockSpec((conv_w.shape[0], tn), lambda i, j: (0, j)),
               pl.BlockSpec((1, tn), lambda i, j: (0, j))],
              jax.ShapeDtypeStruct((b, n, wd), F32),
              pl.BlockSpec((1, n, tn), lambda i, j: (i, 0, j)), name + "_conv")
    return gg, xc


def _lru_scan_kernel(*refs, reverse, reset, final, tt):
    if final:
        xc_ref, wab_ref, bias_ref, lam_ref, h0_ref, prev_ref, gg_ref, o_ref, hfin_ref, a_s, b_s, h_s, st = refs
    else:
        xc_ref, wab_ref, bias_ref, lam_ref, h0_ref, o_ref, hfin_ref, a_s, b_s, h_s, st = refs
    i = pl.program_id(0)
    nb = xc_ref.shape[0]
    ng = wab_ref.shape[0]
    rows = nb * tt

    @pl.when(i == 0)
    def _():
        st[...] = h0_ref[...]

    first_t = tt - 1 if reverse else 0
    row = lax.broadcasted_iota(jnp.int32, (rows, LANES), 0)
    for gi in range(ng):
        xg = xc_ref[:, :, gi * LANES:(gi + 1) * LANES].reshape(rows, LANES)
        pre = jnp.dot(xg.astype(BF16), wab_ref[gi], preferred_element_type=F32) + bias_ref[gi]
        r = jax.nn.sigmoid(pre[:, :LANES])
        ig = jax.nn.sigmoid(pre[:, LANES:])
        sp = jax.nn.softplus(-lam_ref[gi])
        log_a = (-LRU_C) * r * sp
        a = jnp.exp(log_a)
        mult = jnp.sqrt(1.0 - a * a)
        if reset:
            mult = jnp.where(jnp.logical_and(row % tt == first_t, i == 0), 1.0, mult)
        a_s[gi] = a
        b_s[gi] = mult * ig * xg

    def step(s, hs):
        t = (tt - 1 - s) if reverse else s
        out = []
        for gi in range(ng):
            a = a_s[gi, pl.ds(t, nb, stride=tt), :]
            bb = b_s[gi, pl.ds(t, nb, stride=tt), :]
            h = a * hs[gi] + bb
            h_s[gi, pl.ds(t, nb, stride=tt), :] = h
            out.append(h)
        return tuple(out)

    hs = lax.fori_loop(0, tt, step, tuple(st[gi] for gi in range(ng)))
    for gi in range(ng):
        st[gi] = hs[gi]
        hfin_ref[gi] = hs[gi]
        hseq = h_s[gi].reshape(nb, tt, LANES)
        sl = slice(gi * LANES, (gi + 1) * LANES)
        if final:
            o_ref[:, :, sl] = ((prev_ref[:, :, sl] + hseq) * gg_ref[:, :, sl].astype(F32)).astype(BF16)
        else:
            o_ref[:, :, sl] = hseq


def _lru_scan(xc, wab, bias, lam, h0, prev, gg, reverse, reset, name):
    b, n, wd = xc.shape
    ng = wab.shape[0]
    tt = min(n, 32)
    nt = n // tt
    final = prev is not None
    tmap = (lambda i: (0, nt - 1 - i, 0)) if reverse else (lambda i: (0, i, 0))
    full3 = lambda i: (0, 0, 0)
    seq_spec = pl.BlockSpec((b, tt, wd), tmap)
    in_specs = [seq_spec,
                pl.BlockSpec(wab.shape, full3),
                pl.BlockSpec(bias.shape, full3),
                pl.BlockSpec(lam.shape, full3),
                pl.BlockSpec(h0.shape, full3)]
    args = [xc, wab, bias, lam, h0]
    if final:
        in_specs += [seq_spec, seq_spec]
        args += [prev, gg]
    kern = functools.partial(_lru_scan_kernel, reverse=reverse, reset=reset, final=final, tt=tt)
    return pl.pallas_call(
        kern,
        grid=(nt,),
        in_specs=in_specs,
        out_specs=[seq_spec, pl.BlockSpec(h0.shape, full3)],
        out_shape=[jax.ShapeDtypeStruct((b, n, wd), BF16 if final else F32),
                   jax.ShapeDtypeStruct(h0.shape, F32)],
        scratch_shapes=[pltpu.VMEM((ng, b * tt, LANES), F32),
                        pltpu.VMEM((ng, b * tt, LANES), F32),
                        pltpu.VMEM((ng, b * tt, LANES), F32),
                        pltpu.VMEM(h0.shape, F32)],
        compiler_params=_cparams(("arbitrary",)),
        name=name,
    )(*args)


def _window_mean_minus_self(x, window):
    n = x.shape[0]
    back, fwd = x, x
    span = 1
    while span < window // 2:
        back = back + _shift_rows(back, span)
        fwd = fwd + _shift_rows(fwd, -span)
        span *= 2
    p = _shift_rows(back, 1) + fwd
    t = lax.broadcasted_iota(jnp.int32, x.shape, 0)
    lo = jnp.maximum(t - window // 2, 0)
    hi = jnp.minimum(t + window // 2, n)
    cnt = (hi - lo).astype(F32)
    return p / cnt - x


def _pool_epilogue(acc, j, extra, outs):
    wg_ref, ps_ref, xcol_ref, gate_ref = extra
    o_ref = outs[0]
    for gidx, window in enumerate(POOL_WINDOWS):
        @pl.when(j == gidx)
        def _(window=window):
            pooled = _window_mean_minus_self(acc, window).astype(BF16)
            y = jnp.dot(pooled, wg_ref[0], preferred_element_type=F32) * ps_ref[...]
            o_ref[0] = xcol_ref[0] + gate_ref[0] * y


def _pool_mixer(xs, g, shift, scale, w_in, w_grp, pscale, gate, name):
    b, n, d = xs.shape
    ngrp, gd, _ = w_grp.shape
    bi = _bidx(gate.shape[0])
    extra = (w_grp, pscale, xs, gate)
    extra_specs = [pl.BlockSpec((1, gd, gd), lambda i, j: (j, 0, 0)),
                   pl.BlockSpec((1, gd), lambda i, j: (0, j)),
                   pl.BlockSpec((1, n, gd), lambda i, j: (i, 0, j)),
                   pl.BlockSpec((1, 1, gd), lambda i, j: (bi(i), 0, j))]
    return _nmm(xs, g, shift, scale, w_in, gd, _pool_epilogue, extra, extra_specs,
                jax.ShapeDtypeStruct((b, n, d), F32),
                pl.BlockSpec((1, n, gd), lambda i, j: (i, 0, j)), name)


def _route_kernel(x_ref, g_ref, sh_ref, sc_ref, rhi_ref, rlo_ref, u_ref, meta_ref, cnt_ref, *, n_groups):
    u = _norm_mod(x_ref[0], g_ref[...], sh_ref[0], sc_ref[0])
    tm = u.shape[0]
    n_chunks = u.shape[1] // LANES
    for jc in range(n_chunks):
        u_ref[pl.ds(jc, tm, stride=n_chunks), :] = u[:, jc * LANES:(jc + 1) * LANES]
    u_hi = u.astype(BF16)
    u_lo = (u - u_hi.astype(F32)).astype(BF16)
    lg = (jnp.dot(u_hi, rhi_ref[...], preferred_element_type=F32)
          + jnp.dot(u_lo, rhi_ref[...], preferred_element_type=F32)
          + jnp.dot(u_hi, rlo_ref[...], preferred_element_type=F32))
    lane = lax.broadcasted_iota(jnp.int32, lg.shape, 1)
    neg = jnp.float32(-jnp.inf)
    big = jnp.int32(1 << 20)
    is_g = lane < n_groups
    gmax = jnp.max(jnp.where(is_g, lg, neg), axis=-1, keepdims=True)
    g_sel = jnp.min(jnp.where(jnp.logical_and(is_g, lg == gmax), lane, big), axis=-1, keepdims=True)
    p_sel = 1.0 / jnp.sum(jnp.where(is_g, jnp.exp(lg - gmax), 0.0), axis=-1, keepdims=True)
    e_idx = lane - n_groups
    in_grp = jnp.logical_and(e_idx >= g_sel * EXPERTS_PER_GROUP, e_idx < (g_sel + 1) * EXPERTS_PER_GROUP)
    m1 = jnp.max(jnp.where(in_grp, lg, neg), axis=-1, keepdims=True)
    i1 = jnp.min(jnp.where(jnp.logical_and(in_grp, lg == m1), lane, big), axis=-1, keepdims=True)
    rest = jnp.logical_and(in_grp, lane != i1)
    m2 = jnp.max(jnp.where(rest, lg, neg), axis=-1, keepdims=True)
    i2 = jnp.min(jnp.where(jnp.logical_and(rest, lg == m2), lane, big), axis=-1, keepdims=True)
    e2 = jnp.exp(m2 - m1)
    w1 = p_sel / (1.0 + e2)
    w2 = w1 * e2
    off = n_groups + g_sel * EXPERTS_PER_GROUP
    la = jnp.minimum(i1, i2) - off
    lb = jnp.maximum(i1, i2) - off
    pair = ((la * (2 * EXPERTS_PER_GROUP - 1 - la)) >> 1) + lb - la - 1
    cls = g_sel * PAIRS_PER_GROUP + pair
    c_lo = jnp.where(i1 < i2, w1, w2)
    c_hi = jnp.where(i1 < i2, w2, w1)
    onehot = lane == cls
    row = lax.broadcasted_iota(jnp.int32, (tm, tm), 0)
    col = lax.broadcasted_iota(jnp.int32, (tm, tm), 1)
    tri = jnp.where(row > col, 1.0, 0.0).astype(BF16)
    before = jnp.dot(tri, jnp.where(onehot, 1.0, 0.0).astype(BF16), preferred_element_type=F32)
    rank = jnp.sum(jnp.where(onehot, before, 0.0), axis=-1, keepdims=True)
    meta_ref[0] = jnp.where(lane == 0, cls.astype(F32),
                            jnp.where(lane == 1, rank,
                                      jnp.where(lane == 2, c_lo, jnp.where(lane == 3, c_hi, 0.0))))
    cnt = jnp.sum(jnp.where(onehot, 1.0, 0.0), axis=0, keepdims=True)
    cnt_ref[0] = jnp.broadcast_to(cnt, cnt_ref.shape[1:])


def _route(xs, g, shift, scale, r_hi, r_lo, n_groups, name):
    b, n, d = xs.shape
    tm = min(n, ROUTE_BLOCK)
    n_chunks = d // LANES
    nt = n // tm
    bi = _bidx(shift.shape[0])
    kern = functools.partial(_route_kernel, n_groups=n_groups)
    return pl.pallas_call(
        kern,
        grid=(b, nt),
        in_specs=[pl.BlockSpec((1, tm, d), lambda i, t: (i, t, 0)),
                  pl.BlockSpec((1, d), lambda i, t: (0, 0)),
                  pl.BlockSpec((1, 1, d), lambda i, t: (bi(i), 0, 0)),
                  pl.BlockSpec((1, 1, d), lambda i, t: (bi(i), 0, 0)),
                  pl.BlockSpec((d, LANES), lambda i, t: (0, 0)),
                  pl.BlockSpec((d, LANES), lambda i, t: (0, 0))],
        out_specs=[pl.BlockSpec((tm * n_chunks, LANES), lambda i, t: (i * nt + t, 0)),
                   pl.BlockSpec((1, tm, LANES), lambda i, t: (i, t, 0)),
                   pl.BlockSpec((1, 8, LANES), lambda i, t: (i * nt + t, 0, 0))],
        out_shape=[jax.ShapeDtypeStruct((b * n * n_chunks, LANES), F32),
                   jax.ShapeDtypeStruct((b, n, LANES), F32),
                   jax.ShapeDtypeStruct((b * nt, 8, LANES), F32)],
        compiler_params=_cparams(("parallel", "parallel")),
        name=name,
    )(xs, g, shift, scale, r_hi, r_lo)


def _ring_dma(make, n):
    ring = min(DMA_RING, n)

    def start(k, c):
        make(k).start()
        return c

    def turn(k, c):
        make(k - ring).wait()
        make(k).start()
        return c

    def drain(k, c):
        make(k).wait()
        return c

    lax.fori_loop(0, ring, start, 0)
    lax.fori_loop(ring, n, turn, 0)
    lax.fori_loop(n - ring, n, drain, 0)


def _dispatch_kernel(base_ref, cls_ref, rank_ref, u_ref, buf_in_ref, buf_ref, sems, *, blk0, n_cls):
    del buf_in_ref
    tb = cls_ref.shape[-1]
    rows = buf_ref.shape[1]
    boff = (blk0 + pl.program_id(0)) * n_cls

    def make(k):
        pos = base_ref[boff + cls_ref[0, 0, k]] + rank_ref[0, 0, k]
        src = u_ref.at[pl.ds(pl.multiple_of(k * rows, rows), rows)]
        return pltpu.make_async_copy(src, buf_ref.at[pos], sems.at[k % DMA_RING])

    _ring_dma(make, tb)


def _dispatch(base, cls, rank, u_flat, buf, blk0, n_cls, name):
    nb, _, tb = cls.shape
    rows = buf.shape[1]
    smem = lambda: pl.BlockSpec((1, 1, tb), lambda i, bs: (i, 0, 0), memory_space=pltpu.SMEM)
    grid_spec = pltpu.PrefetchScalarGridSpec(
        num_scalar_prefetch=1,
        grid=(nb,),
        in_specs=[smem(), smem(),
                  pl.BlockSpec((tb * rows, LANES), lambda i, bs: (i, 0)),
                  pl.BlockSpec(memory_space=pl.ANY)],
        out_specs=pl.BlockSpec(memory_space=pl.ANY),
        scratch_shapes=[pltpu.SemaphoreType.DMA((DMA_RING,))],
    )
    return pl.pallas_call(
        functools.partial(_dispatch_kernel, blk0=blk0, n_cls=n_cls),
        grid_spec=grid_spec,
        out_shape=jax.ShapeDtypeStruct(buf.shape, buf.dtype),
        input_output_aliases={4: 0},
        compiler_params=pltpu.CompilerParams(dimension_semantics=("arbitrary",), has_side_effects=True),
        name=name,
    )(base, cls, rank, u_flat, buf)


def _experts_kernel(ea_ref, eb_ref, nused_ref, x_ref, w1a_ref, w3a_ref, w2a_ref, w1b_ref, w3b_ref, w2b_ref,
                    y_ref):
    del ea_ref, eb_ref
    k = pl.program_id(0)
    n_chunks = w1a_ref.shape[1] // LANES
    mt = x_ref.shape[0] // n_chunks

    @pl.when(k < nused_ref[0])
    def _():
        x = jnp.concatenate([x_ref[pl.ds(jc, mt, stride=n_chunks), :] for jc in range(n_chunks)],
                            axis=-1).astype(BF16)
        for half, (w1_ref, w3_ref, w2_ref) in enumerate(((w1a_ref, w3a_ref, w2a_ref),
                                                         (w1b_ref, w3b_ref, w2b_ref))):
            h1 = jnp.dot(x, w1_ref[0], preferred_element_type=F32)
            h3 = jnp.dot(x, w3_ref[0], preferred_element_type=F32)
            he = (h1 * jax.nn.sigmoid(h1) * h3).astype(BF16)
            y = jnp.dot(he, w2_ref[0], preferred_element_type=F32)
            for jc in range(n_chunks):
                y_ref[pl.ds(half * n_chunks + jc, mt, stride=2 * n_chunks), :] = y[:, jc * LANES:(jc + 1) * LANES]

    @pl.when(k >= nused_ref[0])
    def _():
        y_ref[...] = jnp.zeros_like(y_ref)


def _experts(tile_ea, tile_eb, n_used, xbuf, w1, w3, w2, mt, name):
    ne, d, f = w1.shape
    n_chunks = d // LANES
    n_tiles = xbuf.shape[0] // (mt * n_chunks)
    wa = lambda k, ea, eb, nu: (ea[k], 0, 0)
    wb = lambda k, ea, eb, nu: (eb[k], 0, 0)
    grid_spec = pltpu.PrefetchScalarGridSpec(
        num_scalar_prefetch=3,
        grid=(n_tiles,),
        in_specs=[pl.BlockSpec((mt * n_chunks, LANES), lambda k, ea, eb, nu: (k, 0)),
                  pl.BlockSpec((1, d, f), wa), pl.BlockSpec((1, d, f), wa), pl.BlockSpec((1, f, d), wa),
                  pl.BlockSpec((1, d, f), wb), pl.BlockSpec((1, d, f), wb), pl.BlockSpec((1, f, d), wb)],
        out_specs=pl.BlockSpec((mt * 2 * n_chunks, LANES), lambda k, ea, eb, nu: (k, 0)),
    )
    return pl.pallas_call(
        _experts_kernel,
        grid_spec=grid_spec,
        out_shape=jax.ShapeDtypeStruct((n_tiles * mt * 2 * n_chunks, LANES), F32),
        compiler_params=_cparams(("arbitrary",)),
        name=name,
    )(tile_ea, tile_eb, n_used, xbuf, w1, w3, w2, w1, w3, w2)


def _combine_kernel(base_ref, cls_ref, rank_ref, y_hbm, meta_ref, x_ref, g_ref, o_ref, buf, sems, *, blk0, n_cls):
    tb = cls_ref.shape[-1]
    rows = y_hbm.shape[1]
    n_chunks = rows // 2
    boff = (blk0 + pl.program_id(0) * pl.num_programs(1) + pl.program_id(1)) * n_cls

    def make(k):
        pos = base_ref[boff + cls_ref[0, 0, k]] + rank_ref[0, 0, k]
        dst = buf.at[pl.ds(pl.multiple_of(k * rows, rows), rows)]
        return pltpu.make_async_copy(y_hbm.at[pos], dst, sems.at[k % DMA_RING])

    _ring_dma(make, tb)
    meta = meta_ref[0]
    c_lo = meta[:, 2:3]
    c_hi = meta[:, 3:4]
    for jc in range(n_chunks):
        sl = slice(jc * LANES, (jc + 1) * LANES)
        y_lo = buf[pl.ds(jc, tb, stride=rows), :]
        y_hi = buf[pl.ds(n_chunks + jc, tb, stride=rows), :]
        o_ref[0, :, sl] = x_ref[0, :, sl] + g_ref[0, :, sl] * (c_lo * y_lo + c_hi * y_hi)


def _combine(base, cls, rank, y3, meta, xs, gate, blk0, n_cls, name):
    b, n, d = xs.shape
    tb = cls.shape[-1]
    nt = n // tb
    rows = y3.shape[1]
    bi = _bidx(gate.shape[0])
    smem = lambda: pl.BlockSpec((1, 1, tb), lambda i, t, bs: (i * nt + t, 0, 0), memory_space=pltpu.SMEM)
    grid_spec = pltpu.PrefetchScalarGridSpec(
        num_scalar_prefetch=1,
        grid=(b, nt),
        in_specs=[smem(), smem(),
                  pl.BlockSpec(memory_space=pl.ANY),
                  pl.BlockSpec((1, tb, LANES), lambda i, t, bs: (i, t, 0)),
                  pl.BlockSpec((1, tb, d), lambda i, t, bs: (i, t, 0)),
                  pl.BlockSpec((1, 1, d), lambda i, t, bs: (bi(i), 0, 0))],
        out_specs=pl.BlockSpec((1, tb, d), lambda i, t, bs: (i, t, 0)),
        scratch_shapes=[pltpu.VMEM((tb * rows, LANES), F32),
                        pltpu.SemaphoreType.DMA((DMA_RING,))],
    )
    return pl.pallas_call(
        functools.partial(_combine_kernel, blk0=blk0, n_cls=n_cls),
        grid_spec=grid_spec,
        out_shape=jax.ShapeDtypeStruct((b, n, d), F32),
        compiler_params=_cparams(("arbitrary", "arbitrary")),
        name=name,
    )(base, cls, rank, y3, meta, xs, gate)


def _hier_moe(streams, g, r_hi, r_lo, w1, w3, w2, n_groups, name):
    d = w1.shape[1]
    n_chunks = d // LANES
    n_cls = n_groups * PAIRS_PER_GROUP
    routed = [_route(xs, g, sh, sc, r_hi, r_lo, n_groups, f"{name}_route{si}")
              for si, (xs, sh, sc, _) in enumerate(streams)]
    cnt = jnp.concatenate([r[2][:, 0, :n_cls] for r in routed], axis=0).astype(jnp.int32)
    total = jnp.sum(cnt, axis=0)
    padded = ((total + MOE_TILE - 1) // MOE_TILE) * MOE_TILE
    cls_end = jnp.cumsum(padded)
    base = (cls_end - padded)[None, :] + jnp.cumsum(cnt, axis=0) - cnt
    n_tok = sum(xs.shape[0] * xs.shape[1] for xs, _, _, _ in streams)
    n_tiles = n_tok // MOE_TILE + n_cls
    tile_start = jnp.arange(n_tiles, dtype=jnp.int32) * MOE_TILE
    n_used = (cls_end[-1] // MOE_TILE).astype(jnp.int32)
    last_start = jnp.maximum(cls_end[-1] - MOE_TILE, 0)
    tile_cls = jnp.sum((jnp.minimum(tile_start, last_start)[:, None] >= cls_end[None, :]).astype(jnp.int32), axis=1)
    tile_cls = jnp.minimum(tile_cls, n_cls - 1)
    pairs = [(a, b) for a in range(EXPERTS_PER_GROUP) for b in range(a + 1, EXPERTS_PER_GROUP)]
    lo_tab = jnp.array([gi * EXPERTS_PER_GROUP + a for gi in range(n_groups) for a, _ in pairs], jnp.int32)
    hi_tab = jnp.array([gi * EXPERTS_PER_GROUP + b for gi in range(n_groups) for _, b in pairs], jnp.int32)
    tile_ea = lo_tab[tile_cls]
    tile_eb = hi_tab[tile_cls]

    base_flat = base.reshape(-1).astype(jnp.int32)
    buf = jnp.zeros((n_tiles * MOE_TILE, n_chunks, LANES), F32)
    idx, blk0 = [], 0
    for si, ((xs, _, _, _), (u_flat, meta, c)) in enumerate(zip(streams, routed)):
        nb = c.shape[0]
        tm = xs.shape[1] * xs.shape[0] // nb
        cls_t = meta[..., 0].astype(jnp.int32).reshape(nb, 1, tm)
        rank_t = meta[..., 1].astype(jnp.int32).reshape(nb, 1, tm)
        idx.append((cls_t, rank_t, blk0))
        buf = _dispatch(base_flat, cls_t, rank_t, u_flat, buf, blk0, n_cls, f"{name}_dispatch{si}")
        blk0 += nb
    y = _experts(tile_ea, tile_eb, n_used.reshape(1), buf.reshape(-1, LANES), w1, w3, w2, MOE_TILE,
                 f"{name}_experts")
    y3 = y.reshape(-1, 2 * n_chunks, LANES)
    outs = []
    for si, ((xs, _, _, gate), (_, meta, _), (cls_t, rank_t, b0)) in enumerate(zip(streams, routed, idx)):
        outs.append(_combine(base_flat, cls_t, rank_t, y3, meta, xs, gate, b0, n_cls, f"{name}_combine{si}"))
    return outs


def _rope_tables(n, dqk):
    rows = n // GRID_W
    row = jnp.repeat(jnp.arange(rows), GRID_W).astype(F32)
    col = jnp.tile(jnp.arange(GRID_W), rows).astype(F32)
    n_freq = dqk // 4
    inv = ROPE_BASE ** (-jnp.arange(n_freq, dtype=F32) / n_freq)
    ang = jnp.concatenate([row[:, None] * inv, col[:, None] * inv], axis=-1)
    cos, sin = jnp.cos(ang), jnp.sin(ang)
    reps = LANES // dqk
    cos_l = jnp.tile(jnp.concatenate([cos, cos], axis=-1), (1, reps))
    sin_l = jnp.tile(jnp.concatenate([-sin, sin], axis=-1), (1, reps))
    return cos_l, sin_l


def kernel(x, c, ctx, c_ctx, w_ada, b_ada, norm_g, attn_w_in, attn_q_gain, attn_k_gain, attn_lam, attn_sub_gain, attn_w_out, lru_w_in, lru_conv_w, lru_conv_b, lru_w_a, lru_b_a, lru_w_x, lru_b_x, lru_lam, lru_w_out, pool_w_in, pool_w_grp, pool_scale, moe_router_g, moe_router_e, moe_w1, moe_w3, moe_w2):
    bsz, seq, d = x.shape
    n_ctx = ctx.shape[1]
    depth = w_ada.shape[0]
    dqk = attn_q_gain.shape[-1]
    heads = d // (2 * dqk)
    n_groups = moe_router_g.shape[-1]
    n_experts = moe_router_e.shape[-1]
    lru_w = lru_w_out.shape[1]
    lru_g = lru_w_a.shape[2]

    pad = (-(bsz + 1)) % 8
    cc = jnp.concatenate([c, c_ctx[None, :], jnp.zeros((pad, d), F32)], axis=0)
    mods = _ada(cc, w_ada, b_ada)

    cos_l, sin_l = _rope_tables(seq, dqk)
    cos_c = jnp.ones((n_ctx, LANES), F32)
    sin_c = jnp.zeros((n_ctx, LANES), F32)
    lane = jnp.arange(LANES)
    gmat = (lane[:, None] // dqk == lane[None, :] // dqk).astype(BF16)

    xs, cs = x, ctx
    for i in range(depth):
        need_ctx = i < depth - 1
        mx = [mods[i, :bsz, k * d:(k + 1) * d][:, None, :] for k in range(6)]
        mc = [mods[i, bsz:bsz + 1, k * d:(k + 1) * d][:, None, :] for k in range(6)]
        g1 = norm_g[i, 0][None, :]
        g2 = norm_g[i, 1][None, :]
        kind, j = i % 3, i // 3
        nm = f"l{i}"
        if kind == 0:
            lam_init = 0.8 - 0.6 * math.exp(-0.3 * i)
            lq = attn_lam[j].astype(F32)
            lam = (jnp.exp(jnp.sum(lq[0] * lq[1])) - jnp.exp(jnp.sum(lq[2] * lq[3])) + lam_init).reshape(1)
            w_in = attn_w_in[j].astype(BF16)
            reps = LANES // dqk
            gains = jnp.stack([jnp.tile(attn_q_gain[j], reps) * (dqk ** -0.5),
                               jnp.tile(attn_k_gain[j], reps)])[:, None, :]
            sub_gain = attn_sub_gain[j][None, :]
            qkv_x = _qkv_proj(xs, g1, mx[0], mx[1], w_in, gains, cos_l, sin_l, gmat, True, dqk, nm + "_qkv_x")
            qkv_c = _qkv_proj(cs, g1, mc[0], mc[1], w_in, gains, cos_c, sin_c, gmat, False, dqk, nm + "_qkv_c")
            w_out = attn_w_out[j].astype(BF16)
            ox = _attention(lam, qkv_x, [qkv_c, qkv_x], sub_gain, heads, dqk, 1.0 - lam_init, nm + "_attn_x")
            xs = _mm_res(ox, w_out, xs, mx[2], nm + "_out_x")
            if need_ctx:
                oc = _attention(lam, qkv_c, [qkv_c], sub_gain, heads, dqk, 1.0 - lam_init, nm + "_attn_c")
                cs = _mm_res(oc, w_out, cs, mc[2], nm + "_out_c")
        elif kind == 1:
            w_in = lru_w_in[j].astype(BF16)
            w_gate, w_xr = w_in[:, :lru_w], w_in[:, lru_w:]
            conv_b = lru_conv_b[j][None, :]
            ggx, xcx = _lru_proj(xs, g1, mx[0], mx[1], w_gate, w_xr, lru_conv_w[j], conv_b, nm + "_proj_x")
            ggc, xcc = _lru_proj(cs, g1, mc[0], mc[1], w_gate, w_xr, lru_conv_w[j], conv_b, nm + "_proj_c")
            h0 = jnp.zeros((lru_g, bsz, LANES), F32)
            hx_prev, hc_prev = None, None
            for dr, reverse in enumerate((False, True)):
                wab = jnp.concatenate([lru_w_a[j, dr], lru_w_x[j, dr]], axis=-1).astype(BF16)
                bias = jnp.concatenate([lru_b_a[j, dr].reshape(lru_g, 1, LANES),
                                        lru_b_x[j, dr].reshape(lru_g, 1, LANES)], axis=-1)
                lam_d = lru_lam[j, dr].reshape(lru_g, 1, LANES)
                last = dr == 1
                hc_seq, hc_fin = _lru_scan(xcc, wab, bias, lam_d, h0, hc_prev if last else None,
                                           ggc if last else None, reverse, True, f"{nm}_scan_c{dr}")
                hx_seq, _ = _lru_scan(xcx, wab, bias, lam_d, hc_fin, hx_prev if last else None,
                                      ggx if last else None, reverse, False, f"{nm}_scan_x{dr}")
                hx_prev, hc_prev = hx_seq, hc_seq
            w_out = lru_w_out[j].astype(BF16)
            xs = _mm_res(hx_prev, w_out, xs, mx[2], nm + "_out_x")
            if need_ctx:
                cs = _mm_res(hc_prev, w_out, cs, mc[2], nm + "_out_c")
        else:
            w_in = pool_w_in[j].astype(BF16)
            w_grp = pool_w_grp[j].astype(BF16)
            pscale = pool_scale[j][None, :]
            xs_new = _pool_mixer(xs, g1, mx[0], mx[1], w_in, w_grp, pscale, mx[2], nm + "_pool_x")
            if need_ctx:
                cs = _pool_mixer(cs, g1, mc[0], mc[1], w_in, w_grp, pscale, mc[2], nm + "_pool_c")
            xs = xs_new

        r_all = jnp.concatenate([moe_router_g[i], moe_router_e[i],
                                 jnp.zeros((d, LANES - n_groups - n_experts), F32)], axis=-1)
        r_hi = r_all.astype(BF16)
        r_lo = (r_all - r_hi.astype(F32)).astype(BF16)
        w1 = moe_w1[i].astype(BF16)
        w3 = moe_w3[i].astype(BF16)
        w2 = moe_w2[i].astype(BF16)
        streams = [(xs, mx[3], mx[4], mx[5])]
        if need_ctx:
            streams.append((cs, mc[3], mc[4], mc[5]))
        outs = _hier_moe(streams, g2, r_hi, r_lo, w1, w3, w2, n_groups, nm + "_moe")
        xs = outs[0]
        if need_ctx:
            cs = outs[1]
    return xs
```

```python
import functools
import math

import jax
import jax.numpy as jnp
from jax import lax
from jax.experimental import pallas as pl
from jax.experimental.pallas import tpu as pltpu

F32 = jnp.float32
BF16 = jnp.bfloat16

EPS = 1e-6
GRID_W = 64
ROPE_BASE = 10000.0
LRU_C = 8.0
POOL_WINDOWS = (2, 4, 8, 16)
EXPERTS_PER_GROUP = 4
PAIRS_PER_GROUP = EXPERTS_PER_GROUP * (EXPERTS_PER_GROUP - 1) // 2
ROUTE_BLOCK = 512
MOE_TILE = 512
DMA_QUEUES = 2
LANES = 128
VMEM_LIMIT = 56 << 20
ROW_CHUNK = 512


def _cparams(sem):
    return pltpu.CompilerParams(dimension_semantics=sem, vmem_limit_bytes=VMEM_LIMIT)


def _norm_mod(x, g, shift, scale):
    ms = jnp.mean(x * x, axis=-1, keepdims=True)
    return (x * lax.rsqrt(ms + EPS) * g) * (1.0 + scale) + shift


def _bidx(bm):
    return (lambda b: b) if bm > 1 else (lambda b: 0)


def _ada_kernel(c_ref, w_ref, b_ref, o_ref):
    c = c_ref[...]
    sc = c * jax.nn.sigmoid(c)
    o_ref[0] = jnp.dot(sc, w_ref[0], preferred_element_type=F32,
                       precision=lax.Precision.HIGHEST) + b_ref[0]


def _ada(cc, w_ada, b_ada):
    depth, d, n6 = w_ada.shape
    r = cc.shape[0]
    tn = 1536
    return pl.pallas_call(
        _ada_kernel,
        grid=(depth, n6 // tn),
        in_specs=[pl.BlockSpec((r, d), lambda i, j: (0, 0)),
                  pl.BlockSpec((1, d, tn), lambda i, j: (i, 0, j)),
                  pl.BlockSpec((1, 1, tn), lambda i, j: (i, 0, j))],
        out_specs=pl.BlockSpec((1, r, tn), lambda i, j: (i, 0, j)),
        out_shape=jax.ShapeDtypeStruct((depth, r, n6), F32),
        compiler_params=_cparams(("parallel", "parallel")),
        name="ada",
    )(cc, w_ada, b_ada.reshape(depth, 1, n6))


def _nmm_kernel(*refs, epilogue, n_extra):
    x_ref, g_ref, sh_ref, sc_ref, w_ref = refs[:5]
    extra = refs[5:5 + n_extra]
    outs = refs[5 + n_extra:-1]
    u_ref = refs[-1]
    j = pl.program_id(1)
    n = x_ref.shape[1]

    @pl.when(j == 0)
    def _():
        for r0 in range(0, n, ROW_CHUNK):
            r1 = min(n, r0 + ROW_CHUNK)
            u_ref[r0:r1, :] = _norm_mod(x_ref[0, r0:r1, :], g_ref[...], sh_ref[0], sc_ref[0]).astype(BF16)

    acc = jnp.dot(u_ref[...], w_ref[...], preferred_element_type=F32)
    epilogue(acc, j, extra, outs)


def _nmm(xs, g, shift, scale, w, tn, epilogue, extra, extra_specs, out_shapes, out_specs, name):
    b, n, d = xs.shape
    bi = _bidx(shift.shape[0])
    nout = w.shape[1]
    kern = functools.partial(_nmm_kernel, epilogue=epilogue, n_extra=len(extra))
    return pl.pallas_call(
        kern,
        grid=(b, nout // tn),
        in_specs=[pl.BlockSpec((1, n, d), lambda i, j: (i, 0, 0)),
                  pl.BlockSpec((1, d), lambda i, j: (0, 0)),
                  pl.BlockSpec((1, 1, d), lambda i, j: (bi(i), 0, 0)),
                  pl.BlockSpec((1, 1, d), lambda i, j: (bi(i), 0, 0)),
                  pl.BlockSpec((d, tn), lambda i, j: (0, j))] + list(extra_specs),
        out_specs=out_specs,
        out_shape=out_shapes,
        scratch_shapes=[pltpu.VMEM((n, d), BF16)],
        compiler_params=_cparams(("parallel", "arbitrary")),
        name=name,
    )(xs, g, shift, scale, w, *extra)


def _qkv_epilogue(acc, j, extra, outs, *, use_rope, n_sec, dqk):
    gain_ref, cos_ref, sin_ref, gmat_ref = extra
    o_ref = outs[0]
    tn = acc.shape[1]

    @pl.when(j < 2 * n_sec)
    def _():
        for hb in range(tn // LANES):
            a = acc[:, hb * LANES:(hb + 1) * LANES]
            ss = jnp.dot((a * a).astype(BF16), gmat_ref[...], preferred_element_type=F32)
            y = a * lax.rsqrt(ss * (1.0 / dqk) + EPS) * gain_ref[0]
            if use_rope:
                lane = lax.broadcasted_iota(jnp.int32, y.shape, 1)
                half = dqk // 2
                partner = jnp.where(lane % dqk < half,
                                    pltpu.roll(y, LANES - half, axis=1),
                                    pltpu.roll(y, half, axis=1))
                y = y * cos_ref[...] + partner * sin_ref[...]
            o_ref[0, :, hb * LANES:(hb + 1) * LANES] = y.astype(BF16)

    @pl.when(j >= 2 * n_sec)
    def _():
        o_ref[0] = acc.astype(BF16)


def _qkv_proj(xs, g, shift, scale, w, gains, cos, sin, gmat, use_rope, dqk, name):
    b, n, d = xs.shape
    tn = 256
    n_sec = d // tn
    epi = functools.partial(_qkv_epilogue, use_rope=use_rope, n_sec=n_sec, dqk=dqk)
    extra = (gains, cos, sin, gmat)
    extra_specs = [pl.BlockSpec((1, 1, LANES), lambda i, j: (jnp.minimum(j // n_sec, 1), 0, 0)),
                   pl.BlockSpec((n, LANES), lambda i, j: (0, 0)),
                   pl.BlockSpec((n, LANES), lambda i, j: (0, 0)),
                   pl.BlockSpec((LANES, LANES), lambda i, j: (0, 0))]
    return _nmm(xs, g, shift, scale, w, tn, epi, extra, extra_specs,
                jax.ShapeDtypeStruct((b, n, 3 * d), BF16),
                pl.BlockSpec((1, n, tn), lambda i, j: (i, 0, j)), name)


def _attn_kernel(lam_ref, q_ref, *refs, n_parts, dqk, post_scale):
    k_refs = refs[:n_parts]
    v_refs = refs[n_parts:2 * n_parts]
    sg_ref = refs[2 * n_parts]
    o_ref = refs[2 * n_parts + 1]
    lam = lam_ref[0]
    q = q_ref[0]
    lane = lax.broadcasted_iota(jnp.int32, q.shape, 1)
    zero = jnp.zeros_like(q)
    qm = (jnp.where(lane < dqk, q, zero), jnp.where(lane >= dqk, q, zero))
    dn = (((1,), (1,)), ((), ()))
    es, inv = [], []
    for m in range(2):
        s = [lax.dot_general(qm[m], k_ref[0], dn, preferred_element_type=F32) for k_ref in k_refs]
        mx = functools.reduce(jnp.maximum, [jnp.max(sp, axis=-1, keepdims=True) for sp in s])
        e = [jnp.exp(sp - mx) for sp in s]
        den = functools.reduce(lambda a, c: a + c, [jnp.sum(ep, axis=-1, keepdims=True) for ep in e])
        es.append(e)
        inv.append(1.0 / den)
    c0 = inv[0]
    c1 = lam * inv[1]
    o = None
    for p in range(n_parts):
        a = (es[0][p] * c0 - es[1][p] * c1).astype(BF16)
        t = jnp.dot(a, v_refs[p][0], preferred_element_type=F32)
        o = t if o is None else o + t
    ms = jnp.mean(o * o, axis=-1, keepdims=True)
    o = o * lax.rsqrt(ms + EPS) * (sg_ref[...] * post_scale)
    o_ref[0] = o.astype(BF16)


def _attention(lam, q_arr, kv_arrs, sub_gain, heads, dqk, post_scale, name):
    b, nq, d3 = q_arr.shape
    d = d3 // 3
    dv = 2 * dqk
    tq = min(nq, 256)
    n_parts = len(kv_arrs)
    kern = functools.partial(_attn_kernel, n_parts=n_parts, dqk=dqk, post_scale=post_scale)
    in_specs = [pl.BlockSpec(memory_space=pltpu.SMEM),
                pl.BlockSpec((1, tq, dv), lambda i, h, t: (i, t, h))]
    in_specs += [pl.BlockSpec((1, a.shape[1], dv), lambda i, h, t: (i, 0, heads + h)) for a in kv_arrs]
    in_specs += [pl.BlockSpec((1, a.shape[1], dv), lambda i, h, t: (i, 0, 2 * heads + h)) for a in kv_arrs]
    in_specs += [pl.BlockSpec((1, dv), lambda i, h, t: (0, 0))]
    return pl.pallas_call(
        kern,
        grid=(b, heads, nq // tq),
        in_specs=in_specs,
        out_specs=pl.BlockSpec((1, tq, dv), lambda i, h, t: (i, t, h)),
        out_shape=jax.ShapeDtypeStruct((b, nq, d), BF16),
        compiler_params=_cparams(("parallel", "parallel", "arbitrary")),
        name=name,
    )(lam, q_arr, *kv_arrs, *kv_arrs, sub_gain)


def _mm_res_kernel(a_ref, w_ref, x_ref, g_ref, o_ref):
    y = jnp.dot(a_ref[0], w_ref[...], preferred_element_type=F32)
    o_ref[0] = x_ref[0] + g_ref[0] * y


def _mm_res(a, w, xs, gate, name):
    b, n, k = a.shape
    d = w.shape[1]
    tm = min(n, 512)
    bi = _bidx(gate.shape[0])
    return pl.pallas_call(
        _mm_res_kernel,
        grid=(b, n // tm),
        in_specs=[pl.BlockSpec((1, tm, k), lambda i, t: (i, t, 0)),
                  pl.BlockSpec((k, d), lambda i, t: (0, 0)),
                  pl.BlockSpec((1, tm, d), lambda i, t: (i, t, 0)),
                  pl.BlockSpec((1, 1, d), lambda i, t: (bi(i), 0, 0))],
        out_specs=pl.BlockSpec((1, tm, d), lambda i, t: (i, t, 0)),
        out_shape=jax.ShapeDtypeStruct((b, n, d), F32),
        compiler_params=_cparams(("parallel", "parallel")),
        name=name,
    )(a, w, xs, gate)


def _gelu_epilogue(acc, j, extra, outs):
    outs[0][0] = jax.nn.gelu(acc).astype(BF16)


def _shift_rows(x, s):
    n = x.shape[0]
    row = lax.broadcasted_iota(jnp.int32, x.shape, 0)
    r = pltpu.roll(x, s % n, axis=0)
    if s >= 0:
        return jnp.where(row >= s, r, 0.0)
    return jnp.where(row < n + s, r, 0.0)


def _conv_epilogue(acc, j, extra, outs):
    cw_ref, cb_ref = extra
    width = cw_ref.shape[0]
    left = width // 2
    y = cb_ref[...] + jnp.zeros_like(acc)
    for k in range(width):
        y = y + cw_ref[k:k + 1, :] * _shift_rows(acc, left - k)
    outs[0][0] = y


def _lru_proj(xs, g, shift, scale, w_gate, w_xr, conv_w, conv_b, name):
    b, n, d = xs.shape
    wd = w_gate.shape[1]
    tn = 256
    gg = _nmm(xs, g, shift, scale, w_gate, tn, _gelu_epilogue, (), [],
              jax.ShapeDtypeStruct((b, n, wd), BF16),
              pl.BlockSpec((1, n, tn), lambda i, j: (i, 0, j)), name + "_gate")
    xc = _nmm(xs, g, shift, scale, w_xr, tn, _conv_epilogue, (conv_w, conv_b),
              [pl.BlockSpec((conv_w.shape[0], tn), lambda i, j: (0, j)),
               pl.BlockSpec((1, tn), lambda i, j: (0, j))],
              jax.ShapeDtypeStruct((b, n, wd), F32),
              pl.BlockSpec((1, n, tn), lambda i, j: (i, 0, j)), name + "_conv")
    return gg, xc


def _lru_scan_kernel(*refs, reverse, reset, final, tt):
    if final:
        xc_ref, wab_ref, bias_ref, lam_ref, h0_ref, prev_ref, gg_ref, o_ref, hfin_ref, a_s, b_s, h_s, st = refs
    else:
        xc_ref, wab_ref, bias_ref, lam_ref, h0_ref, o_ref, hfin_ref, a_s, b_s, h_s, st = refs
    i = pl.program_id(0)
    nb = xc_ref.shape[0]
    ng = wab_ref.shape[0]
    rows = nb * tt

    @pl.when(i == 0)
    def _():
        st[...] = h0_ref[...]

    first_t = tt - 1 if reverse else 0
    row = lax.broadcasted_iota(jnp.int32, (rows, LANES), 0)
    for gi in range(ng):
        xg = xc_ref[:, :, gi * LANES:(gi + 1) * LANES].reshape(rows, LANES)
        pre = jnp.dot(xg.astype(BF16), wab_ref[gi], preferred_element_type=F32) + bias_ref[gi]
        r = jax.nn.sigmoid(pre[:, :LANES])
        ig = jax.nn.sigmoid(pre[:, LANES:])
        sp = jax.nn.softplus(-lam_ref[gi])
        log_a = (-LRU_C) * r * sp
        a = jnp.exp(log_a)
        mult = jnp.sqrt(1.0 - a * a)
        if reset:
            mult = jnp.where(jnp.logical_and(row % tt == first_t, i == 0), 1.0, mult)
        a_s[gi] = a
        b_s[gi] = mult * ig * xg

    def step(s, hs):
        t = (tt - 1 - s) if reverse else s
        out = []
        for gi in range(ng):
            a = a_s[gi, pl.ds(t, nb, stride=tt), :]
            bb = b_s[gi, pl.ds(t, nb, stride=tt), :]
            h = a * hs[gi] + bb
            h_s[gi, pl.ds(t, nb, stride=tt), :] = h
            out.append(h)
        return tuple(out)

    hs = lax.fori_loop(0, tt, step, tuple(st[gi] for gi in range(ng)))
    for gi in range(ng):
        st[gi] = hs[gi]
        hfin_ref[gi] = hs[gi]
        hseq = h_s[gi].reshape(nb, tt, LANES)
        sl = slice(gi * LANES, (gi + 1) * LANES)
        if final:
            o_ref[:, :, sl] = ((prev_ref[:, :, sl] + hseq) * gg_ref[:, :, sl].astype(F32)).astype(BF16)
        else:
            o_ref[:, :, sl] = hseq


def _lru_scan(xc, wab, bias, lam, h0, prev, gg, reverse, reset, name):
    b, n, wd = xc.shape
    ng = wab.shape[0]
    tt = min(n, 32)
    nt = n // tt
    final = prev is not None
    tmap = (lambda i: (0, nt - 1 - i, 0)) if reverse else (lambda i: (0, i, 0))
    full3 = lambda i: (0, 0, 0)
    seq_spec = pl.BlockSpec((b, tt, wd), tmap)
    in_specs = [seq_spec,
                pl.BlockSpec(wab.shape, full3),
                pl.BlockSpec(bias.shape, full3),
                pl.BlockSpec(lam.shape, full3),
                pl.BlockSpec(h0.shape, full3)]
    args = [xc, wab, bias, lam, h0]
    if final:
        in_specs += [seq_spec, seq_spec]
        args += [prev, gg]
    kern = functools.partial(_lru_scan_kernel, reverse=reverse, reset=reset, final=final, tt=tt)
    return pl.pallas_call(
        kern,
        grid=(nt,),
        in_specs=in_specs,
        out_specs=[seq_spec, pl.BlockSpec(h0.shape, full3)],
        out_shape=[jax.ShapeDtypeStruct((b, n, wd), BF16 if final else F32),
                   jax.ShapeDtypeStruct(h0.shape, F32)],
        scratch_shapes=[pltpu.VMEM((ng, b * tt, LANES), F32),
                        pltpu.VMEM((ng, b * tt, LANES), F32),
                        pltpu.VMEM((ng, b * tt, LANES), F32),
                        pltpu.VMEM(h0.shape, F32)],
        compiler_params=_cparams(("arbitrary",)),
        name=name,
    )(*args)


def _window_mean_minus_self(x, window):
    n = x.shape[0]
    back, fwd = x, x
    span = 1
    while span < window // 2:
        back = back + _shift_rows(back, span)
        fwd = fwd + _shift_rows(fwd, -span)
        span *= 2
    p = _shift_rows(back, 1) + fwd
    t = lax.broadcasted_iota(jnp.int32, x.shape, 0)
    lo = jnp.maximum(t - window // 2, 0)
    hi = jnp.minimum(t + window // 2, n)
    cnt = (hi - lo).astype(F32)
    return p / cnt - x


def _pool_epilogue(acc, j, extra, outs):
    wg_ref, ps_ref, xcol_ref, gate_ref = extra
    o_ref = outs[0]
    for gidx, window in enumerate(POOL_WINDOWS):
        @pl.when(j == gidx)
        def _(window=window):
            pooled = _window_mean_minus_self(acc, window).astype(BF16)
            y = jnp.dot(pooled, wg_ref[0], preferred_element_type=F32) * ps_ref[...]
            o_ref[0] = xcol_ref[0] + gate_ref[0] * y


def _pool_mixer(xs, g, shift, scale, w_in, w_grp, pscale, gate, name):
    b, n, d = xs.shape
    ngrp, gd, _ = w_grp.shape
    bi = _bidx(gate.shape[0])
    extra = (w_grp, pscale, xs, gate)
    extra_specs = [pl.BlockSpec((1, gd, gd), lambda i, j: (j, 0, 0)),
                   pl.BlockSpec((1, gd), lambda i, j: (0, j)),
                   pl.BlockSpec((1, n, gd), lambda i, j: (i, 0, j)),
                   pl.BlockSpec((1, 1, gd), lambda i, j: (bi(i), 0, j))]
    return _nmm(xs, g, shift, scale, w_in, gd, _pool_epilogue, extra, extra_specs,
                jax.ShapeDtypeStruct((b, n, d), F32),
                pl.BlockSpec((1, n, gd), lambda i, j: (i, 0, j)), name)


def _route_kernel(x_ref, g_ref, sh_ref, sc_ref, rhi_ref, rlo_ref, u_ref, meta_ref, cnt_ref, *, n_groups):
    u = _norm_mod(x_ref[0], g_ref[...], sh_ref[0], sc_ref[0])
    tm = u.shape[0]
    n_chunks = u.shape[1] // LANES
    for jc in range(n_chunks):
        u_ref[pl.ds(jc, tm, stride=n_chunks), :] = u[:, jc * LANES:(jc + 1) * LANES]
    u_hi = u.astype(BF16)
    u_lo = (u - u_hi.astype(F32)).astype(BF16)
    lg = (jnp.dot(u_hi, rhi_ref[...], preferred_element_type=F32)
          + jnp.dot(u_lo, rhi_ref[...], preferred_element_type=F32)
          + jnp.dot(u_hi, rlo_ref[...], preferred_element_type=F32))
    lane = lax.broadcasted_iota(jnp.int32, lg.shape, 1)
    neg = jnp.float32(-jnp.inf)
    big = jnp.int32(1 << 20)
    is_g = lane < n_groups
    gmax = jnp.max(jnp.where(is_g, lg, neg), axis=-1, keepdims=True)
    g_sel = jnp.min(jnp.where(jnp.logical_and(is_g, lg == gmax), lane, big), axis=-1, keepdims=True)
    p_sel = 1.0 / jnp.sum(jnp.where(is_g, jnp.exp(lg - gmax), 0.0), axis=-1, keepdims=True)
    e_idx = lane - n_groups
    in_grp = jnp.logical_and(e_idx >= g_sel * EXPERTS_PER_GROUP, e_idx < (g_sel + 1) * EXPERTS_PER_GROUP)
    m1 = jnp.max(jnp.where(in_grp, lg, neg), axis=-1, keepdims=True)
    i1 = jnp.min(jnp.where(jnp.logical_and(in_grp, lg == m1), lane, big), axis=-1, keepdims=True)
    rest = jnp.logical_and(in_grp, lane != i1)
    m2 = jnp.max(jnp.where(rest, lg, neg), axis=-1, keepdims=True)
    i2 = jnp.min(jnp.where(jnp.logical_and(rest, lg == m2), lane, big), axis=-1, keepdims=True)
    e2 = jnp.exp(m2 - m1)
    w1 = p_sel / (1.0 + e2)
    w2 = w1 * e2
    off = n_groups + g_sel * EXPERTS_PER_GROUP
    la = jnp.minimum(i1, i2) - off
    lb = jnp.maximum(i1, i2) - off
    pair = ((la * (2 * EXPERTS_PER_GROUP - 1 - la)) >> 1) + lb - la - 1
    cls = g_sel * PAIRS_PER_GROUP + pair
    c_lo = jnp.where(i1 < i2, w1, w2)
    c_hi = jnp.where(i1 < i2, w2, w1)
    onehot = lane == cls
    row = lax.broadcasted_iota(jnp.int32, (tm, tm), 0)
    col = lax.broadcasted_iota(jnp.int32, (tm, tm), 1)
    tri = jnp.where(row > col, 1.0, 0.0).astype(BF16)
    before = jnp.dot(tri, jnp.where(onehot, 1.0, 0.0).astype(BF16), preferred_element_type=F32)
    rank = jnp.sum(jnp.where(onehot, before, 0.0), axis=-1, keepdims=True)
    meta_ref[0] = jnp.where(lane == 0, cls.astype(F32),
                            jnp.where(lane == 1, rank,
                                      jnp.where(lane == 2, c_lo, jnp.where(lane == 3, c_hi, 0.0))))
    cnt = jnp.sum(jnp.where(onehot, 1.0, 0.0), axis=0, keepdims=True)
    cnt_ref[0] = jnp.broadcast_to(cnt, cnt_ref.shape[1:])


def _route(xs, g, shift, scale, r_hi, r_lo, n_groups, name):
    b, n, d = xs.shape
    tm = min(n, ROUTE_BLOCK)
    n_chunks = d // LANES
    nt = n // tm
    bi = _bidx(shift.shape[0])
    kern = functools.partial(_route_kernel, n_groups=n_groups)
    return pl.pallas_call(
        kern,
        grid=(b, nt),
        in_specs=[pl.BlockSpec((1, tm, d), lambda i, t: (i, t, 0)),
                  pl.BlockSpec((1, d), lambda i, t: (0, 0)),
                  pl.BlockSpec((1, 1, d), lambda i, t: (bi(i), 0, 0)),
                  pl.BlockSpec((1, 1, d), lambda i, t: (bi(i), 0, 0)),
                  pl.BlockSpec((d, LANES), lambda i, t: (0, 0)),
                  pl.BlockSpec((d, LANES), lambda i, t: (0, 0))],
        out_specs=[pl.BlockSpec((tm * n_chunks, LANES), lambda i, t: (i * nt + t, 0)),
                   pl.BlockSpec((1, tm, LANES), lambda i, t: (i, t, 0)),
                   pl.BlockSpec((1, 8, LANES), lambda i, t: (i * nt + t, 0, 0))],
        out_shape=[jax.ShapeDtypeStruct((b * n * n_chunks, LANES), F32),
                   jax.ShapeDtypeStruct((b, n, LANES), F32),
                   jax.ShapeDtypeStruct((b * nt, 8, LANES), F32)],
        compiler_params=_cparams(("parallel", "parallel")),
        name=name,
    )(xs, g, shift, scale, r_hi, r_lo)


def _burst_dma(make, n, wait_all):
    def start(i, c):
        for pr in range(2):
            make(2 * i + pr, pr).start(priority=pr)
        return c

    lax.fori_loop(0, n // 2, start, 0)
    for pr in range(2):
        wait_all(pr)


def _dispatch_kernel(base_ref, cls_ref, rank_ref, u_ref, buf_in_ref, buf_ref, sems, *, blk0, n_cls):
    del buf_in_ref
    tb = cls_ref.shape[-1]
    rows = buf_ref.shape[1]
    boff = (blk0 + pl.program_id(0)) * n_cls

    def make(k, slot):
        pos = base_ref[boff + cls_ref[0, 0, k]] + rank_ref[0, 0, k]
        src = u_ref.at[pl.ds(pl.multiple_of(k * rows, rows), rows)]
        return pltpu.make_async_copy(src, buf_ref.at[pos], sems.at[slot])

    def wait_all(slot):
        half = buf_ref.at[pl.ds(0, tb // 2)]
        pltpu.make_async_copy(half, half, sems.at[slot]).wait()

    _burst_dma(make, tb, wait_all)


def _dispatch(base, cls, rank, u_flat, buf, blk0, n_cls, name):
    nb, _, tb = cls.shape
    rows = buf.shape[1]
    smem = lambda: pl.BlockSpec((1, 1, tb), lambda i, bs: (i, 0, 0), memory_space=pltpu.SMEM)
    grid_spec = pltpu.PrefetchScalarGridSpec(
        num_scalar_prefetch=1,
        grid=(nb,),
        in_specs=[smem(), smem(),
                  pl.BlockSpec((tb * rows, LANES), lambda i, bs: (i, 0)),
                  pl.BlockSpec(memory_space=pl.ANY)],
        out_specs=pl.BlockSpec(memory_space=pl.ANY),
        scratch_shapes=[pltpu.SemaphoreType.DMA((DMA_QUEUES,))],
    )
    return pl.pallas_call(
        functools.partial(_dispatch_kernel, blk0=blk0, n_cls=n_cls),
        grid_spec=grid_spec,
        out_shape=jax.ShapeDtypeStruct(buf.shape, buf.dtype),
        input_output_aliases={4: 0},
        compiler_params=pltpu.CompilerParams(dimension_semantics=("arbitrary",), has_side_effects=True),
        name=name,
    )(base, cls, rank, u_flat, buf)


def _experts_kernel(ea_ref, eb_ref, nused_ref, x_ref, w1a_ref, w3a_ref, w2a_ref, w1b_ref, w3b_ref, w2b_ref,
                    y_ref):
    del ea_ref, eb_ref
    k = pl.program_id(0)
    n_chunks = w1a_ref.shape[1] // LANES
    mt = x_ref.shape[0] // n_chunks

    @pl.when(k < nused_ref[0])
    def _():
        x = jnp.concatenate([x_ref[pl.ds(jc, mt, stride=n_chunks), :] for jc in range(n_chunks)],
                            axis=-1).astype(BF16)
        for half, (w1_ref, w3_ref, w2_ref) in enumerate(((w1a_ref, w3a_ref, w2a_ref),
                                                         (w1b_ref, w3b_ref, w2b_ref))):
            h1 = jnp.dot(x, w1_ref[0], preferred_element_type=F32)
            h3 = jnp.dot(x, w3_ref[0], preferred_element_type=F32)
            he = (h1 * jax.nn.sigmoid(h1) * h3).astype(BF16)
            y = jnp.dot(he, w2_ref[0], preferred_element_type=F32)
            for jc in range(n_chunks):
                y_ref[pl.ds(half * n_chunks + jc, mt, stride=2 * n_chunks), :] = y[:, jc * LANES:(jc + 1) * LANES]

    @pl.when(k >= nused_ref[0])
    def _():
        y_ref[...] = jnp.zeros_like(y_ref)


def _experts(tile_ea, tile_eb, n_used, xbuf, w1, w3, w2, mt, name):
    ne, d, f = w1.shape
    n_chunks = d // LANES
    n_tiles = xbuf.shape[0] // (mt * n_chunks)
    wa = lambda k, ea, eb, nu: (ea[k], 0, 0)
    wb = lambda k, ea, eb, nu: (eb[k], 0, 0)
    grid_spec = pltpu.PrefetchScalarGridSpec(
        num_scalar_prefetch=3,
        grid=(n_tiles,),
        in_specs=[pl.BlockSpec((mt * n_chunks, LANES), lambda k, ea, eb, nu: (k, 0)),
                  pl.BlockSpec((1, d, f), wa), pl.BlockSpec((1, d, f), wa), pl.BlockSpec((1, f, d), wa),
                  pl.BlockSpec((1, d, f), wb), pl.BlockSpec((1, d, f), wb), pl.BlockSpec((1, f, d), wb)],
        out_specs=pl.BlockSpec((mt * 2 * n_chunks, LANES), lambda k, ea, eb, nu: (k, 0)),
    )
    return pl.pallas_call(
        _experts_kernel,
        grid_spec=grid_spec,
        out_shape=jax.ShapeDtypeStruct((n_tiles * mt * 2 * n_chunks, LANES), F32),
        compiler_params=_cparams(("arbitrary",)),
        name=name,
    )(tile_ea, tile_eb, n_used, xbuf, w1, w3, w2, w1, w3, w2)


def _combine_kernel(base_ref, cls_ref, rank_ref, y_hbm, meta_ref, x_ref, g_ref, o_ref, buf, sems, *, blk0, n_cls):
    tb = cls_ref.shape[-1]
    rows = y_hbm.shape[1]
    n_chunks = rows // 2
    boff = (blk0 + pl.program_id(0) * pl.num_programs(1) + pl.program_id(1)) * n_cls

    def make(k, slot):
        pos = base_ref[boff + cls_ref[0, 0, k]] + rank_ref[0, 0, k]
        dst = buf.at[pl.ds(pl.multiple_of(k * rows, rows), rows)]
        return pltpu.make_async_copy(y_hbm.at[pos], dst, sems.at[slot])

    def wait_all(slot):
        half = buf.at[pl.ds(0, (tb // 2) * rows)]
        pltpu.make_async_copy(half, half, sems.at[slot]).wait()

    _burst_dma(make, tb, wait_all)
    meta = meta_ref[0]
    c_lo = meta[:, 2:3]
    c_hi = meta[:, 3:4]
    for jc in range(n_chunks):
        sl = slice(jc * LANES, (jc + 1) * LANES)
        y_lo = buf[pl.ds(jc, tb, stride=rows), :]
        y_hi = buf[pl.ds(n_chunks + jc, tb, stride=rows), :]
        o_ref[0, :, sl] = x_ref[0, :, sl] + g_ref[0, :, sl] * (c_lo * y_lo + c_hi * y_hi)


def _combine(base, cls, rank, y3, meta, xs, gate, blk0, n_cls, name):
    b, n, d = xs.shape
    tb = cls.shape[-1]
    nt = n // tb
    rows = y3.shape[1]
    bi = _bidx(gate.shape[0])
    smem = lambda: pl.BlockSpec((1, 1, tb), lambda i, t, bs: (i * nt + t, 0, 0), memory_space=pltpu.SMEM)
    grid_spec = pltpu.PrefetchScalarGridSpec(
        num_scalar_prefetch=1,
        grid=(b, nt),
        in_specs=[smem(), smem(),
                  pl.BlockSpec(memory_space=pl.ANY),
                  pl.BlockSpec((1, tb, LANES), lambda i, t, bs: (i, t, 0)),
                  pl.BlockSpec((1, tb, d), lambda i, t, bs: (i, t, 0)),
                  pl.BlockSpec((1, 1, d), lambda i, t, bs: (bi(i), 0, 0))],
        out_specs=pl.BlockSpec((1, tb, d), lambda i, t, bs: (i, t, 0)),
        scratch_shapes=[pltpu.VMEM((tb * rows, LANES), F32),
                        pltpu.SemaphoreType.DMA((DMA_QUEUES,))],
    )
    return pl.pallas_call(
        functools.partial(_combine_kernel, blk0=blk0, n_cls=n_cls),
        grid_spec=grid_spec,
        out_shape=jax.ShapeDtypeStruct((b, n, d), F32),
        compiler_params=_cparams(("arbitrary", "arbitrary")),
        name=name,
    )(base, cls, rank, y3, meta, xs, gate)


def _hier_moe(streams, g, r_hi, r_lo, w1, w3, w2, n_groups, name):
    d = w1.shape[1]
    n_chunks = d // LANES
    n_cls = n_groups * PAIRS_PER_GROUP
    routed = [_route(xs, g, sh, sc, r_hi, r_lo, n_groups, f"{name}_route{si}")
              for si, (xs, sh, sc, _) in enumerate(streams)]
    cnt = jnp.concatenate([r[2][:, 0, :n_cls] for r in routed], axis=0).astype(jnp.int32)
    total = jnp.sum(cnt, axis=0)
    padded = ((total + MOE_TILE - 1) // MOE_TILE) * MOE_TILE
    cls_end = jnp.cumsum(padded)
    base = (cls_end - padded)[None, :] + jnp.cumsum(cnt, axis=0) - cnt
    n_tok = sum(xs.shape[0] * xs.shape[1] for xs, _, _, _ in streams)
    n_tiles = n_tok // MOE_TILE + n_cls
    tile_start = jnp.arange(n_tiles, dtype=jnp.int32) * MOE_TILE
    n_used = (cls_end[-1] // MOE_TILE).astype(jnp.int32)
    last_start = jnp.maximum(cls_end[-1] - MOE_TILE, 0)
    tile_cls = jnp.sum((jnp.minimum(tile_start, last_start)[:, None] >= cls_end[None, :]).astype(jnp.int32), axis=1)
    tile_cls = jnp.minimum(tile_cls, n_cls - 1)
    pairs = [(a, b) for a in range(EXPERTS_PER_GROUP) for b in range(a + 1, EXPERTS_PER_GROUP)]
    lo_tab = jnp.array([gi * EXPERTS_PER_GROUP + a for gi in range(n_groups) for a, _ in pairs], jnp.int32)
    hi_tab = jnp.array([gi * EXPERTS_PER_GROUP + b for gi in range(n_groups) for _, b in pairs], jnp.int32)
    tile_ea = lo_tab[tile_cls]
    tile_eb = hi_tab[tile_cls]

    base_flat = base.reshape(-1).astype(jnp.int32)
    buf = jnp.zeros((n_tiles * MOE_TILE, n_chunks, LANES), F32)
    idx, blk0 = [], 0
    for si, ((xs, _, _, _), (u_flat, meta, c)) in enumerate(zip(streams, routed)):
        nb = c.shape[0]
        tm = xs.shape[1] * xs.shape[0] // nb
        cls_t = meta[..., 0].astype(jnp.int32).reshape(nb, 1, tm)
        rank_t = meta[..., 1].astype(jnp.int32).reshape(nb, 1, tm)
        idx.append((cls_t, rank_t, blk0))
        buf = _dispatch(base_flat, cls_t, rank_t, u_flat, buf, blk0, n_cls, f"{name}_dispatch{si}")
        blk0 += nb
    y = _experts(tile_ea, tile_eb, n_used.reshape(1), buf.reshape(-1, LANES), w1, w3, w2, MOE_TILE,
                 f"{name}_experts")
    y3 = y.reshape(-1, 2 * n_chunks, LANES)
    outs = []
    for si, ((xs, _, _, gate), (_, meta, _), (cls_t, rank_t, b0)) in enumerate(zip(streams, routed, idx)):
        outs.append(_combine(base_flat, cls_t, rank_t, y3, meta, xs, gate, b0, n_cls, f"{name}_combine{si}"))
    return outs


def _rope_tables(n, dqk):
    rows = n // GRID_W
    row = jnp.repeat(jnp.arange(rows), GRID_W).astype(F32)
    col = jnp.tile(jnp.arange(GRID_W), rows).astype(F32)
    n_freq = dqk // 4
    inv = ROPE_BASE ** (-jnp.arange(n_freq, dtype=F32) / n_freq)
    ang = jnp.concatenate([row[:, None] * inv, col[:, None] * inv], axis=-1)
    cos, sin = jnp.cos(ang), jnp.sin(ang)
    reps = LANES // dqk
    cos_l = jnp.tile(jnp.concatenate([cos, cos], axis=-1), (1, reps))
    sin_l = jnp.tile(jnp.concatenate([-sin, sin], axis=-1), (1, reps))
    return cos_l, sin_l


def kernel(x, c, ctx, c_ctx, w_ada, b_ada, norm_g, attn_w_in, attn_q_gain, attn_k_gain, attn_lam, attn_sub_gain, attn_w_out, lru_w_in, lru_conv_w, lru_conv_b, lru_w_a, lru_b_a, lru_w_x, lru_b_x, lru_lam, lru_w_out, pool_w_in, pool_w_grp, pool_scale, moe_router_g, moe_router_e, moe_w1, moe_w3, moe_w2):
    bsz, seq, d = x.shape
    n_ctx = ctx.shape[1]
    depth = w_ada.shape[0]
    dqk = attn_q_gain.shape[-1]
    heads = d // (2 * dqk)
    n_groups = moe_router_g.shape[-1]
    n_experts = moe_router_e.shape[-1]
    lru_w = lru_w_out.shape[1]
    lru_g = lru_w_a.shape[2]

    pad = (-(bsz + 1)) % 8
    cc = jnp.concatenate([c, c_ctx[None, :], jnp.zeros((pad, d), F32)], axis=0)
    mods = _ada(cc, w_ada, b_ada)

    cos_l, sin_l = _rope_tables(seq, dqk)
    cos_c = jnp.ones((n_ctx, LANES), F32)
    sin_c = jnp.zeros((n_ctx, LANES), F32)
    lane = jnp.arange(LANES)
    gmat = (lane[:, None] // dqk == lane[None, :] // dqk).astype(BF16)

    xs, cs = x, ctx
    for i in range(depth):
        need_ctx = i < depth - 1
        mx = [mods[i, :bsz, k * d:(k + 1) * d][:, None, :] for k in range(6)]
        mc = [mods[i, bsz:bsz + 1, k * d:(k + 1) * d][:, None, :] for k in range(6)]
        g1 = norm_g[i, 0][None, :]
        g2 = norm_g[i, 1][None, :]
        kind, j = i % 3, i // 3
        nm = f"l{i}"
        if kind == 0:
            lam_init = 0.8 - 0.6 * math.exp(-0.3 * i)
            lq = attn_lam[j].astype(F32)
            lam = (jnp.exp(jnp.sum(lq[0] * lq[1])) - jnp.exp(jnp.sum(lq[2] * lq[3])) + lam_init).reshape(1)
            w_in = attn_w_in[j].astype(BF16)
            reps = LANES // dqk
            gains = jnp.stack([jnp.tile(attn_q_gain[j], reps) * (dqk ** -0.5),
                               jnp.tile(attn_k_gain[j], reps)])[:, None, :]
            sub_gain = attn_sub_gain[j][None, :]
            qkv_x = _qkv_proj(xs, g1, mx[0], mx[1], w_in, gains, cos_l, sin_l, gmat, True, dqk, nm + "_qkv_x")
            qkv_c = _qkv_proj(cs, g1, mc[0], mc[1], w_in, gains, cos_c, sin_c, gmat, False, dqk, nm + "_qkv_c")
            w_out = attn_w_out[j].astype(BF16)
            ox = _attention(lam, qkv_x, [qkv_c, qkv_x], sub_gain, heads, dqk, 1.0 - lam_init, nm + "_attn_x")
            xs = _mm_res(ox, w_out, xs, mx[2], nm + "_out_x")
            if need_ctx:
                oc = _attention(lam, qkv_c, [qkv_c], sub_gain, heads, dqk, 1.0 - lam_init, nm + "_attn_c")
                cs = _mm_res(oc, w_out, cs, mc[2], nm + "_out_c")
        elif kind == 1:
            w_in = lru_w_in[j].astype(BF16)
            w_gate, w_xr = w_in[:, :lru_w], w_in[:, lru_w:]
            conv_b = lru_conv_b[j][None, :]
            ggx, xcx = _lru_proj(xs, g1, mx[0], mx[1], w_gate, w_xr, lru_conv_w[j], conv_b, nm + "_proj_x")
            ggc, xcc = _lru_proj(cs, g1, mc[0], mc[1], w_gate, w_xr, lru_conv_w[j], conv_b, nm + "_proj_c")
            h0 = jnp.zeros((lru_g, bsz, LANES), F32)
            hx_prev, hc_prev = None, None
            for dr, reverse in enumerate((False, True)):
                wab = jnp.concatenate([lru_w_a[j, dr], lru_w_x[j, dr]], axis=-1).astype(BF16)
                bias = jnp.concatenate([lru_b_a[j, dr].reshape(lru_g, 1, LANES),
                                        lru_b_x[j, dr].reshape(lru_g, 1, LANES)], axis=-1)
                lam_d = lru_lam[j, dr].reshape(lru_g, 1, LANES)
                last = dr == 1
                hc_seq, hc_fin = _lru_scan(xcc, wab, bias, lam_d, h0, hc_prev if last else None,
                                           ggc if last else None, reverse, True, f"{nm}_scan_c{dr}")
                hx_seq, _ = _lru_scan(xcx, wab, bias, lam_d, hc_fin, hx_prev if last else None,
                                      ggx if last else None, reverse, False, f"{nm}_scan_x{dr}")
                hx_prev, hc_prev = hx_seq, hc_seq
            w_out = lru_w_out[j].astype(BF16)
            xs = _mm_res(hx_prev, w_out, xs, mx[2], nm + "_out_x")
            if need_ctx:
                cs = _mm_res(hc_prev, w_out, cs, mc[2], nm + "_out_c")
        else:
            w_in = pool_w_in[j].astype(BF16)
            w_grp = pool_w_grp[j].astype(BF16)
            pscale = pool_scale[j][None, :]
            xs_new = _pool_mixer(xs, g1, mx[0], mx[1], w_in, w_grp, pscale, mx[2], nm + "_pool_x")
            if need_ctx:
                cs = _pool_mixer(cs, g1, mc[0], mc[1], w_in, w_grp, pscale, mc[2], nm + "_pool_c")
            xs = xs_new

        r_all = jnp.concatenate([moe_router_g[i], moe_router_e[i],
                                 jnp.zeros((d, LANES - n_groups - n_experts), F32)], axis=-1)
        r_hi = r_all.astype(BF16)
        r_lo = (r_all - r_hi.astype(F32)).astype(BF16)
        w1 = moe_w1[i].astype(BF16)
        w3 = moe_w3[i].astype(BF16)
        w2 = moe_w2[i].astype(BF16)
        streams = [(xs, mx[3], mx[4], mx[5])]
        if need_ctx:
            streams.append((cs, mc[3], mc[4], mc[5]))
        outs = _hier_moe(streams, g2, r_hi, r_lo, w1, w3, w2, n_groups, nm + "_moe")
        xs = outs[0]
        if need_ctx:
            cs = outs[1]
    return xs
```

```python
import functools
import math

import jax
import jax.numpy as jnp
from jax import lax
from jax.experimental import pallas as pl
from jax.experimental.pallas import tpu as pltpu

F32 = jnp.float32
BF16 = jnp.bfloat16

EPS = 1e-6
GRID_W = 64
ROPE_BASE = 10000.0
LRU_C = 8.0
POOL_WINDOWS = (2, 4, 8, 16)
EXPERTS_PER_GROUP = 4
PAIRS_PER_GROUP = EXPERTS_PER_GROUP * (EXPERTS_PER_GROUP - 1) // 2
ROUTE_BLOCK = 512
MOE_TILE = 512
SCAN_ROW_PAD = 8
ATTN_Q_TILE = 512
DMA_QUEUES = 2
LANES = 128
VMEM_LIMIT = 56 << 20
ROW_CHUNK = 512


def _cparams(sem):
    return pltpu.CompilerParams(dimension_semantics=sem, vmem_limit_bytes=VMEM_LIMIT)


def _norm_mod(x, g, shift, scale):
    ms = jnp.mean(x * x, axis=-1, keepdims=True)
    return (x * lax.rsqrt(ms + EPS) * g) * (1.0 + scale) + shift


def _bidx(bm):
    return (lambda b: b) if bm > 1 else (lambda b: 0)


def _ada_kernel(c_ref, w_ref, b_ref, o_ref):
    c = c_ref[...]
    sc = c * jax.nn.sigmoid(c)
    o_ref[0] = jnp.dot(sc, w_ref[0], preferred_element_type=F32,
                       precision=lax.Precision.HIGHEST) + b_ref[0]


def _ada(cc, w_ada, b_ada):
    depth, d, n6 = w_ada.shape
    r = cc.shape[0]
    tn = 1536
    return pl.pallas_call(
        _ada_kernel,
        grid=(depth, n6 // tn),
        in_specs=[pl.BlockSpec((r, d), lambda i, j: (0, 0)),
                  pl.BlockSpec((1, d, tn), lambda i, j: (i, 0, j)),
                  pl.BlockSpec((1, 1, tn), lambda i, j: (i, 0, j))],
        out_specs=pl.BlockSpec((1, r, tn), lambda i, j: (i, 0, j)),
        out_shape=jax.ShapeDtypeStruct((depth, r, n6), F32),
        compiler_params=_cparams(("parallel", "parallel")),
        name="ada",
    )(cc, w_ada, b_ada.reshape(depth, 1, n6))


def _nmm_kernel(*refs, epilogue, n_extra):
    x_ref, g_ref, sh_ref, sc_ref, w_ref = refs[:5]
    extra = refs[5:5 + n_extra]
    outs = refs[5 + n_extra:-1]
    u_ref = refs[-1]
    j = pl.program_id(1)
    n = x_ref.shape[1]

    @pl.when(j == 0)
    def _():
        for r0 in range(0, n, ROW_CHUNK):
            r1 = min(n, r0 + ROW_CHUNK)
            u_ref[r0:r1, :] = _norm_mod(x_ref[0, r0:r1, :], g_ref[...], sh_ref[0], sc_ref[0]).astype(BF16)

    acc = jnp.dot(u_ref[...], w_ref[...], preferred_element_type=F32)
    epilogue(acc, j, extra, outs)


def _nmm(xs, g, shift, scale, w, tn, epilogue, extra, extra_specs, out_shapes, out_specs, name):
    b, n, d = xs.shape
    bi = _bidx(shift.shape[0])
    nout = w.shape[1]
    kern = functools.partial(_nmm_kernel, epilogue=epilogue, n_extra=len(extra))
    return pl.pallas_call(
        kern,
        grid=(b, nout // tn),
        in_specs=[pl.BlockSpec((1, n, d), lambda i, j: (i, 0, 0)),
                  pl.BlockSpec((1, d), lambda i, j: (0, 0)),
                  pl.BlockSpec((1, 1, d), lambda i, j: (bi(i), 0, 0)),
                  pl.BlockSpec((1, 1, d), lambda i, j: (bi(i), 0, 0)),
                  pl.BlockSpec((d, tn), lambda i, j: (0, j))] + list(extra_specs),
        out_specs=out_specs,
        out_shape=out_shapes,
        scratch_shapes=[pltpu.VMEM((n, d), BF16)],
        compiler_params=_cparams(("parallel", "arbitrary")),
        name=name,
    )(xs, g, shift, scale, w, *extra)


def _qkv_epilogue(acc, j, extra, outs, *, use_rope, n_sec, dqk):
    gain_ref, cos_ref, sin_ref, gmat_ref = extra
    o_ref = outs[0]
    tn = acc.shape[1]

    @pl.when(j < 2 * n_sec)
    def _():
        for hb in range(tn // LANES):
            a = acc[:, hb * LANES:(hb + 1) * LANES]
            ss = jnp.dot((a * a).astype(BF16), gmat_ref[...], preferred_element_type=F32)
            y = a * lax.rsqrt(ss * (1.0 / dqk) + EPS) * gain_ref[0]
            if use_rope:
                lane = lax.broadcasted_iota(jnp.int32, y.shape, 1)
                half = dqk // 2
                partner = jnp.where(lane % dqk < half,
                                    pltpu.roll(y, LANES - half, axis=1),
                                    pltpu.roll(y, half, axis=1))
                y = y * cos_ref[...] + partner * sin_ref[...]
            o_ref[0, :, hb * LANES:(hb + 1) * LANES] = y.astype(BF16)

    @pl.when(j >= 2 * n_sec)
    def _():
        o_ref[0] = acc.astype(BF16)


def _qkv_proj(xs, g, shift, scale, w, gains, cos, sin, gmat, use_rope, dqk, name):
    b, n, d = xs.shape
    tn = 256
    n_sec = d // tn
    epi = functools.partial(_qkv_epilogue, use_rope=use_rope, n_sec=n_sec, dqk=dqk)
    extra = (gains, cos, sin, gmat)
    extra_specs = [pl.BlockSpec((1, 1, LANES), lambda i, j: (jnp.minimum(j // n_sec, 1), 0, 0)),
                   pl.BlockSpec((n, LANES), lambda i, j: (0, 0)),
                   pl.BlockSpec((n, LANES), lambda i, j: (0, 0)),
                   pl.BlockSpec((LANES, LANES), lambda i, j: (0, 0))]
    return _nmm(xs, g, shift, scale, w, tn, epi, extra, extra_specs,
                jax.ShapeDtypeStruct((b, n, 3 * d), BF16),
                pl.BlockSpec((1, n, tn), lambda i, j: (i, 0, j)), name)


def _attn_kernel(lam_ref, q_ref, *refs, n_parts, dqk, post_scale):
    k_refs = refs[:n_parts]
    v_refs = refs[n_parts:2 * n_parts]
    sg_ref = refs[2 * n_parts]
    o_ref = refs[2 * n_parts + 1]
    lam = lam_ref[0]
    q = q_ref[0]
    lane = lax.broadcasted_iota(jnp.int32, q.shape, 1)
    zero = jnp.zeros_like(q)
    qm = (jnp.where(lane < dqk, q, zero), jnp.where(lane >= dqk, q, zero))
    dn = (((1,), (1,)), ((), ()))
    es, inv = [], []
    for m in range(2):
        s = [lax.dot_general(qm[m], k_ref[0], dn, preferred_element_type=F32) for k_ref in k_refs]
        mx = functools.reduce(jnp.maximum, [jnp.max(sp, axis=-1, keepdims=True) for sp in s])
        e = [jnp.exp2(sp - mx) for sp in s]
        den = functools.reduce(lambda a, c: a + c, [jnp.sum(ep, axis=-1, keepdims=True) for ep in e])
        es.append(e)
        inv.append(1.0 / den)
    c0 = inv[0]
    c1 = lam * inv[1]
    o = None
    for p in range(n_parts):
        a = (es[0][p] * c0 - es[1][p] * c1).astype(BF16)
        t = jnp.dot(a, v_refs[p][0], preferred_element_type=F32)
        o = t if o is None else o + t
    ms = jnp.mean(o * o, axis=-1, keepdims=True)
    o = o * lax.rsqrt(ms + EPS) * (sg_ref[...] * post_scale)
    o_ref[0] = o.astype(BF16)


def _attention(lam, q_arr, kv_arrs, sub_gain, heads, dqk, post_scale, name):
    b, nq, d3 = q_arr.shape
    d = d3 // 3
    dv = 2 * dqk
    tq = min(nq, ATTN_Q_TILE)
    n_parts = len(kv_arrs)
    kern = functools.partial(_attn_kernel, n_parts=n_parts, dqk=dqk, post_scale=post_scale)
    in_specs = [pl.BlockSpec(memory_space=pltpu.SMEM),
                pl.BlockSpec((1, tq, dv), lambda i, h, t: (i, t, h))]
    in_specs += [pl.BlockSpec((1, a.shape[1], dv), lambda i, h, t: (i, 0, heads + h)) for a in kv_arrs]
    in_specs += [pl.BlockSpec((1, a.shape[1], dv), lambda i, h, t: (i, 0, 2 * heads + h)) for a in kv_arrs]
    in_specs += [pl.BlockSpec((1, dv), lambda i, h, t: (0, 0))]
    return pl.pallas_call(
        kern,
        grid=(b, heads, nq // tq),
        in_specs=in_specs,
        out_specs=pl.BlockSpec((1, tq, dv), lambda i, h, t: (i, t, h)),
        out_shape=jax.ShapeDtypeStruct((b, nq, d), BF16),
        compiler_params=_cparams(("parallel", "parallel", "arbitrary")),
        name=name,
    )(lam, q_arr, *kv_arrs, *kv_arrs, sub_gain)


def _mm_res_kernel(a_ref, w_ref, x_ref, g_ref, o_ref):
    y = jnp.dot(a_ref[0], w_ref[...], preferred_element_type=F32)
    o_ref[0] = x_ref[0] + g_ref[0] * y


def _mm_res(a, w, xs, gate, name):
    b, n, k = a.shape
    d = w.shape[1]
    tm = min(n, 512)
    bi = _bidx(gate.shape[0])
    return pl.pallas_call(
        _mm_res_kernel,
        grid=(b, n // tm),
        in_specs=[pl.BlockSpec((1, tm, k), lambda i, t: (i, t, 0)),
                  pl.BlockSpec((k, d), lambda i, t: (0, 0)),
                  pl.BlockSpec((1, tm, d), lambda i, t: (i, t, 0)),
                  pl.BlockSpec((1, 1, d), lambda i, t: (bi(i), 0, 0))],
        out_specs=pl.BlockSpec((1, tm, d), lambda i, t: (i, t, 0)),
        out_shape=jax.ShapeDtypeStruct((b, n, d), F32),
        compiler_params=_cparams(("parallel", "parallel")),
        name=name,
    )(a, w, xs, gate)


def _gelu_epilogue(acc, j, extra, outs):
    outs[0][0] = jax.nn.gelu(acc).astype(BF16)


def _shift_rows(x, s):
    n = x.shape[0]
    row = lax.broadcasted_iota(jnp.int32, x.shape, 0)
    r = pltpu.roll(x, s % n, axis=0)
    if s >= 0:
        return jnp.where(row >= s, r, 0.0)
    return jnp.where(row < n + s, r, 0.0)


def _conv_epilogue(acc, j, extra, outs):
    cw_ref, cb_ref = extra
    width = cw_ref.shape[0]
    left = width // 2
    y = cb_ref[...] + jnp.zeros_like(acc)
    for k in range(width):
        y = y + cw_ref[k:k + 1, :] * _shift_rows(acc, left - k)
    outs[0][0] = y


def _lru_proj(xs, g, shift, scale, w_gate, w_xr, conv_w, conv_b, name):
    b, n, d = xs.shape
    wd = w_gate.shape[1]
    tn = 256
    gg = _nmm(xs, g, shift, scale, w_gate, tn, _gelu_epilogue, (), [],
              jax.ShapeDtypeStruct((b, n, wd), BF16),
              pl.BlockSpec((1, n, tn), lambda i, j: (i, 0, j)), name + "_gate")
    xc = _nmm(xs, g, shift, scale, w_xr, tn, _conv_epilogue, (conv_w, conv_b),
              [pl.BlockSpec((conv_w.shape[0], tn), lambda i, j: (0, j)),
               pl.BlockSpec((1, tn), lambda i, j: (0, j))],
              jax.ShapeDtypeStruct((b, n, wd), F32),
              pl.BlockSpec((1, n, tn), lambda i, j: (i, 0, j)), name + "_conv")
    return gg, xc


def _lru_scan_kernel(*refs, reverse, reset, final, tt):
    if final:
        xc_ref, wab_ref, bias_ref, lam_ref, h0_ref, prev_ref, gg_ref, o_ref, hfin_ref, a_s, b_s, h_s, st = refs
    else:
        xc_ref, wab_ref, bias_ref, lam_ref, h0_ref, o_ref, hfin_ref, a_s, b_s, h_s, st = refs
    i = pl.program_id(0)
    nb = xc_ref.shape[0]
    ng = wab_ref.shape[0]
    rows = nb * tt
    ts = tt + SCAN_ROW_PAD

    @pl.when(i == 0)
    def _():
        st[...] = h0_ref[...]

    first_t = tt - 1 if reverse else 0
    row = lax.broadcasted_iota(jnp.int32, (rows, LANES), 0)
    for gi in range(ng):
        xg = xc_ref[:, :, gi * LANES:(gi + 1) * LANES].reshape(rows, LANES)
        pre = jnp.dot(xg.astype(BF16), wab_ref[gi], preferred_element_type=F32) + bias_ref[gi]
        r = jax.nn.sigmoid(pre[:, :LANES])
        ig = jax.nn.sigmoid(pre[:, LANES:])
        sp = jax.nn.softplus(-lam_ref[gi])
        log_a = (-LRU_C) * r * sp
        a = jnp.exp(log_a)
        mult = jnp.sqrt(1.0 - a * a)
        if reset:
            mult = jnp.where(jnp.logical_and(row % tt == first_t, i == 0), 1.0, mult)
        bb = mult * ig * xg
        for bi in range(nb):
            a_s[gi, bi * ts:bi * ts + tt, :] = a[bi * tt:(bi + 1) * tt]
            b_s[gi, bi * ts:bi * ts + tt, :] = bb[bi * tt:(bi + 1) * tt]

    def step(s, hs):
        t = (tt - 1 - s) if reverse else s
        out = []
        for gi in range(ng):
            a = a_s[gi, pl.ds(t, nb, stride=ts), :]
            bb = b_s[gi, pl.ds(t, nb, stride=ts), :]
            h = a * hs[gi] + bb
            h_s[gi, pl.ds(t, nb, stride=ts), :] = h
            out.append(h)
        return tuple(out)

    hs = lax.fori_loop(0, tt, step, tuple(st[gi] for gi in range(ng)))
    for gi in range(ng):
        st[gi] = hs[gi]
        hfin_ref[gi] = hs[gi]
        sl = slice(gi * LANES, (gi + 1) * LANES)
        for bi in range(nb):
            hseq = h_s[gi, bi * ts:bi * ts + tt, :]
            if final:
                o_ref[bi, :, sl] = ((prev_ref[bi, :, sl] + hseq) * gg_ref[bi, :, sl].astype(F32)).astype(BF16)
            else:
                o_ref[bi, :, sl] = hseq


def _lru_scan(xc, wab, bias, lam, h0, prev, gg, reverse, reset, name):
    b, n, wd = xc.shape
    ng = wab.shape[0]
    tt = min(n, 32)
    nt = n // tt
    final = prev is not None
    tmap = (lambda i: (0, nt - 1 - i, 0)) if reverse else (lambda i: (0, i, 0))
    full3 = lambda i: (0, 0, 0)
    seq_spec = pl.BlockSpec((b, tt, wd), tmap)
    in_specs = [seq_spec,
                pl.BlockSpec(wab.shape, full3),
                pl.BlockSpec(bias.shape, full3),
                pl.BlockSpec(lam.shape, full3),
                pl.BlockSpec(h0.shape, full3)]
    args = [xc, wab, bias, lam, h0]
    if final:
        in_specs += [seq_spec, seq_spec]
        args += [prev, gg]
    kern = functools.partial(_lru_scan_kernel, reverse=reverse, reset=reset, final=final, tt=tt)
    return pl.pallas_call(
        kern,
        grid=(nt,),
        in_specs=in_specs,
        out_specs=[seq_spec, pl.BlockSpec(h0.shape, full3)],
        out_shape=[jax.ShapeDtypeStruct((b, n, wd), BF16 if final else F32),
                   jax.ShapeDtypeStruct(h0.shape, F32)],
        scratch_shapes=[pltpu.VMEM((ng, b * (tt + SCAN_ROW_PAD), LANES), F32),
                        pltpu.VMEM((ng, b * (tt + SCAN_ROW_PAD), LANES), F32),
                        pltpu.VMEM((ng, b * (tt + SCAN_ROW_PAD), LANES), F32),
                        pltpu.VMEM(h0.shape, F32)],
        compiler_params=_cparams(("arbitrary",)),
        name=name,
    )(*args)


def _window_mean_minus_self(x, window):
    n = x.shape[0]
    back, fwd = x, x
    span = 1
    while span < window // 2:
        back = back + _shift_rows(back, span)
        fwd = fwd + _shift_rows(fwd, -span)
        span *= 2
    p = _shift_rows(back, 1) + fwd
    t = lax.broadcasted_iota(jnp.int32, x.shape, 0)
    lo = jnp.maximum(t - window // 2, 0)
    hi = jnp.minimum(t + window // 2, n)
    cnt = (hi - lo).astype(F32)
    return p / cnt - x


def _pool_epilogue(acc, j, extra, outs):
    wg_ref, ps_ref, xcol_ref, gate_ref = extra
    o_ref = outs[0]
    for gidx, window in enumerate(POOL_WINDOWS):
        @pl.when(j == gidx)
        def _(window=window):
            pooled = _window_mean_minus_self(acc, window).astype(BF16)
            y = jnp.dot(pooled, wg_ref[0], preferred_element_type=F32) * ps_ref[...]
            o_ref[0] = xcol_ref[0] + gate_ref[0] * y


def _pool_mixer(xs, g, shift, scale, w_in, w_grp, pscale, gate, name):
    b, n, d = xs.shape
    ngrp, gd, _ = w_grp.shape
    bi = _bidx(gate.shape[0])
    extra = (w_grp, pscale, xs, gate)
    extra_specs = [pl.BlockSpec((1, gd, gd), lambda i, j: (j, 0, 0)),
                   pl.BlockSpec((1, gd), lambda i, j: (0, j)),
                   pl.BlockSpec((1, n, gd), lambda i, j: (i, 0, j)),
                   pl.BlockSpec((1, 1, gd), lambda i, j: (bi(i), 0, j))]
    return _nmm(xs, g, shift, scale, w_in, gd, _pool_epilogue, extra, extra_specs,
                jax.ShapeDtypeStruct((b, n, d), F32),
                pl.BlockSpec((1, n, gd), lambda i, j: (i, 0, j)), name)


def _route_kernel(x_ref, g_ref, sh_ref, sc_ref, rhi_ref, rlo_ref, u_ref, meta_ref, cnt_ref, *, n_groups):
    u = _norm_mod(x_ref[0], g_ref[...], sh_ref[0], sc_ref[0])
    tm = u.shape[0]
    n_chunks = u.shape[1] // LANES
    for jc in range(n_chunks):
        u_ref[pl.ds(jc, tm, stride=n_chunks), :] = u[:, jc * LANES:(jc + 1) * LANES]
    u_hi = u.astype(BF16)
    u_lo = (u - u_hi.astype(F32)).astype(BF16)
    lg = (jnp.dot(u_hi, rhi_ref[...], preferred_element_type=F32)
          + jnp.dot(u_lo, rhi_ref[...], preferred_element_type=F32)
          + jnp.dot(u_hi, rlo_ref[...], preferred_element_type=F32))
    lane = lax.broadcasted_iota(jnp.int32, lg.shape, 1)
    neg = jnp.float32(-jnp.inf)
    big = jnp.int32(1 << 20)
    is_g = lane < n_groups
    gmax = jnp.max(jnp.where(is_g, lg, neg), axis=-1, keepdims=True)
    g_sel = jnp.min(jnp.where(jnp.logical_and(is_g, lg == gmax), lane, big), axis=-1, keepdims=True)
    p_sel = 1.0 / jnp.sum(jnp.where(is_g, jnp.exp(lg - gmax), 0.0), axis=-1, keepdims=True)
    e_idx = lane - n_groups
    in_grp = jnp.logical_and(e_idx >= g_sel * EXPERTS_PER_GROUP, e_idx < (g_sel + 1) * EXPERTS_PER_GROUP)
    m1 = jnp.max(jnp.where(in_grp, lg, neg), axis=-1, keepdims=True)
    i1 = jnp.min(jnp.where(jnp.logical_and(in_grp, lg == m1), lane, big), axis=-1, keepdims=True)
    rest = jnp.logical_and(in_grp, lane != i1)
    m2 = jnp.max(jnp.where(rest, lg, neg), axis=-1, keepdims=True)
    i2 = jnp.min(jnp.where(jnp.logical_and(rest, lg == m2), lane, big), axis=-1, keepdims=True)
    e2 = jnp.exp(m2 - m1)
    w1 = p_sel / (1.0 + e2)
    w2 = w1 * e2
    off = n_groups + g_sel * EXPERTS_PER_GROUP
    la = jnp.minimum(i1, i2) - off
    lb = jnp.maximum(i1, i2) - off
    pair = ((la * (2 * EXPERTS_PER_GROUP - 1 - la)) >> 1) + lb - la - 1
    cls = g_sel * PAIRS_PER_GROUP + pair
    c_lo = jnp.where(i1 < i2, w1, w2)
    c_hi = jnp.where(i1 < i2, w2, w1)
    onehot = lane == cls
    row = lax.broadcasted_iota(jnp.int32, (tm, tm), 0)
    col = lax.broadcasted_iota(jnp.int32, (tm, tm), 1)
    tri = jnp.where(row > col, 1.0, 0.0).astype(BF16)
    before = jnp.dot(tri, jnp.where(onehot, 1.0, 0.0).astype(BF16), preferred_element_type=F32)
    rank = jnp.sum(jnp.where(onehot, before, 0.0), axis=-1, keepdims=True)
    meta_ref[0] = jnp.where(lane == 0, cls.astype(F32),
                            jnp.where(lane == 1, rank,
                                      jnp.where(lane == 2, c_lo, jnp.where(lane == 3, c_hi, 0.0))))
    cnt = jnp.sum(jnp.where(onehot, 1.0, 0.0), axis=0, keepdims=True)
    cnt_ref[0] = jnp.broadcast_to(cnt, cnt_ref.shape[1:])


def _route(xs, g, shift, scale, r_hi, r_lo, n_groups, name):
    b, n, d = xs.shape
    tm = min(n, ROUTE_BLOCK)
    n_chunks = d // LANES
    nt = n // tm
    bi = _bidx(shift.shape[0])
    kern = functools.partial(_route_kernel, n_groups=n_groups)
    return pl.pallas_call(
        kern,
        grid=(b, nt),
        in_specs=[pl.BlockSpec((1, tm, d), lambda i, t: (i, t, 0)),
                  pl.BlockSpec((1, d), lambda i, t: (0, 0)),
                  pl.BlockSpec((1, 1, d), lambda i, t: (bi(i), 0, 0)),
                  pl.BlockSpec((1, 1, d), lambda i, t: (bi(i), 0, 0)),
                  pl.BlockSpec((d, LANES), lambda i, t: (0, 0)),
                  pl.BlockSpec((d, LANES), lambda i, t: (0, 0))],
        out_specs=[pl.BlockSpec((tm * n_chunks, LANES), lambda i, t: (i * nt + t, 0)),
                   pl.BlockSpec((1, tm, LANES), lambda i, t: (i, t, 0)),
                   pl.BlockSpec((1, 8, LANES), lambda i, t: (i * nt + t, 0, 0))],
        out_shape=[jax.ShapeDtypeStruct((b * n * n_chunks, LANES), F32),
                   jax.ShapeDtypeStruct((b, n, LANES), F32),
                   jax.ShapeDtypeStruct((b * nt, 8, LANES), F32)],
        compiler_params=_cparams(("parallel", "parallel")),
        name=name,
    )(xs, g, shift, scale, r_hi, r_lo)


def _burst_dma(make, n, wait_all):
    def start(i, c):
        for pr in range(2):
            make(2 * i + pr, pr).start(priority=pr)
        return c

    lax.fori_loop(0, n // 2, start, 0)
    for pr in range(2):
        wait_all(pr)


def _dispatch_kernel(base_ref, cls_ref, rank_ref, u_ref, buf_in_ref, buf_ref, sems, *, blk0, n_cls):
    del buf_in_ref
    tb = cls_ref.shape[-1]
    rows = buf_ref.shape[1]
    boff = (blk0 + pl.program_id(0)) * n_cls

    def make(k, slot):
        pos = base_ref[boff + cls_ref[0, 0, k]] + rank_ref[0, 0, k]
        src = u_ref.at[pl.ds(pl.multiple_of(k * rows, rows), rows)]
        return pltpu.make_async_copy(src, buf_ref.at[pos], sems.at[slot])

    def wait_all(slot):
        half = buf_ref.at[pl.ds(0, tb // 2)]
        pltpu.make_async_copy(half, half, sems.at[slot]).wait()

    _burst_dma(make, tb, wait_all)


def _dispatch(base, cls, rank, u_flat, buf, blk0, n_cls, name):
    nb, _, tb = cls.shape
    rows = buf.shape[1]
    smem = lambda: pl.BlockSpec((1, 1, tb), lambda i, bs: (i, 0, 0), memory_space=pltpu.SMEM)
    grid_spec = pltpu.PrefetchScalarGridSpec(
        num_scalar_prefetch=1,
        grid=(nb,),
        in_specs=[smem(), smem(),
                  pl.BlockSpec((tb * rows, LANES), lambda i, bs: (i, 0)),
                  pl.BlockSpec(memory_space=pl.ANY)],
        out_specs=pl.BlockSpec(memory_space=pl.ANY),
        scratch_shapes=[pltpu.SemaphoreType.DMA((DMA_QUEUES,))],
    )
    return pl.pallas_call(
        functools.partial(_dispatch_kernel, blk0=blk0, n_cls=n_cls),
        grid_spec=grid_spec,
        out_shape=jax.ShapeDtypeStruct(buf.shape, buf.dtype),
        input_output_aliases={4: 0},
        compiler_params=pltpu.CompilerParams(dimension_semantics=("arbitrary",), has_side_effects=True),
        name=name,
    )(base, cls, rank, u_flat, buf)


def _experts_kernel(ea_ref, eb_ref, nused_ref, x_ref, w1a_ref, w3a_ref, w2a_ref, w1b_ref, w3b_ref, w2b_ref,
                    y_ref):
    del ea_ref, eb_ref
    k = pl.program_id(0)
    n_chunks = w1a_ref.shape[1] // LANES
    mt = x_ref.shape[0] // n_chunks

    @pl.when(k < nused_ref[0])
    def _():
        x = jnp.concatenate([x_ref[pl.ds(jc, mt, stride=n_chunks), :] for jc in range(n_chunks)],
                            axis=-1).astype(BF16)
        for half, (w1_ref, w3_ref, w2_ref) in enumerate(((w1a_ref, w3a_ref, w2a_ref),
                                                         (w1b_ref, w3b_ref, w2b_ref))):
            h1 = jnp.dot(x, w1_ref[0], preferred_element_type=F32)
            h3 = jnp.dot(x, w3_ref[0], preferred_element_type=F32)
            he = (h1 * jax.nn.sigmoid(h1) * h3).astype(BF16)
            y = jnp.dot(he, w2_ref[0], preferred_element_type=F32)
            for jc in range(n_chunks):
                y_ref[pl.ds(half * n_chunks + jc, mt, stride=2 * n_chunks), :] = y[:, jc * LANES:(jc + 1) * LANES]

    @pl.when(k >= nused_ref[0])
    def _():
        y_ref[...] = jnp.zeros_like(y_ref)


def _experts(tile_ea, tile_eb, n_used, xbuf, w1, w3, w2, mt, name):
    ne, d, f = w1.shape
    n_chunks = d // LANES
    n_tiles = xbuf.shape[0] // (mt * n_chunks)
    wa = lambda k, ea, eb, nu: (ea[k], 0, 0)
    wb = lambda k, ea, eb, nu: (eb[k], 0, 0)
    grid_spec = pltpu.PrefetchScalarGridSpec(
        num_scalar_prefetch=3,
        grid=(n_tiles,),
        in_specs=[pl.BlockSpec((mt * n_chunks, LANES), lambda k, ea, eb, nu: (k, 0)),
                  pl.BlockSpec((1, d, f), wa), pl.BlockSpec((1, d, f), wa), pl.BlockSpec((1, f, d), wa),
                  pl.BlockSpec((1, d, f), wb), pl.BlockSpec((1, d, f), wb), pl.BlockSpec((1, f, d), wb)],
        out_specs=pl.BlockSpec((mt * 2 * n_chunks, LANES), lambda k, ea, eb, nu: (k, 0)),
    )
    return pl.pallas_call(
        _experts_kernel,
        grid_spec=grid_spec,
        out_shape=jax.ShapeDtypeStruct((n_tiles * mt * 2 * n_chunks, LANES), F32),
        compiler_params=_cparams(("arbitrary",)),
        name=name,
    )(tile_ea, tile_eb, n_used, xbuf, w1, w3, w2, w1, w3, w2)


def _combine_kernel(base_ref, cls_ref, rank_ref, y_hbm, meta_ref, x_ref, g_ref, o_ref, buf, sems, *, blk0, n_cls):
    tb = cls_ref.shape[-1]
    rows = y_hbm.shape[1]
    n_chunks = rows // 2
    boff = (blk0 + pl.program_id(0) * pl.num_programs(1) + pl.program_id(1)) * n_cls

    def make(k, slot):
        pos = base_ref[boff + cls_ref[0, 0, k]] + rank_ref[0, 0, k]
        dst = buf.at[pl.ds(pl.multiple_of(k * rows, rows), rows)]
        return pltpu.make_async_copy(y_hbm.at[pos], dst, sems.at[slot])

    def wait_all(slot):
        half = buf.at[pl.ds(0, (tb // 2) * rows)]
        pltpu.make_async_copy(half, half, sems.at[slot]).wait()

    _burst_dma(make, tb, wait_all)
    meta = meta_ref[0]
    c_lo = meta[:, 2:3]
    c_hi = meta[:, 3:4]
    for jc in range(n_chunks):
        sl = slice(jc * LANES, (jc + 1) * LANES)
        y_lo = buf[pl.ds(jc, tb, stride=rows), :]
        y_hi = buf[pl.ds(n_chunks + jc, tb, stride=rows), :]
        o_ref[0, :, sl] = x_ref[0, :, sl] + g_ref[0, :, sl] * (c_lo * y_lo + c_hi * y_hi)


def _combine(base, cls, rank, y3, meta, xs, gate, blk0, n_cls, name):
    b, n, d = xs.shape
    tb = cls.shape[-1]
    nt = n // tb
    rows = y3.shape[1]
    bi = _bidx(gate.shape[0])
    smem = lambda: pl.BlockSpec((1, 1, tb), lambda i, t, bs: (i * nt + t, 0, 0), memory_space=pltpu.SMEM)
    grid_spec = pltpu.PrefetchScalarGridSpec(
        num_scalar_prefetch=1,
        grid=(b, nt),
        in_specs=[smem(), smem(),
                  pl.BlockSpec(memory_space=pl.ANY),
                  pl.BlockSpec((1, tb, LANES), lambda i, t, bs: (i, t, 0)),
                  pl.BlockSpec((1, tb, d), lambda i, t, bs: (i, t, 0)),
                  pl.BlockSpec((1, 1, d), lambda i, t, bs: (bi(i), 0, 0))],
        out_specs=pl.BlockSpec((1, tb, d), lambda i, t, bs: (i, t, 0)),
        scratch_shapes=[pltpu.VMEM((tb * rows, LANES), F32),
                        pltpu.SemaphoreType.DMA((DMA_QUEUES,))],
    )
    return pl.pallas_call(
        functools.partial(_combine_kernel, blk0=blk0, n_cls=n_cls),
        grid_spec=grid_spec,
        out_shape=jax.ShapeDtypeStruct((b, n, d), F32),
        compiler_params=_cparams(("arbitrary", "arbitrary")),
        name=name,
    )(base, cls, rank, y3, meta, xs, gate)


def _hier_moe(streams, g, r_hi, r_lo, w1, w3, w2, n_groups, name):
    d = w1.shape[1]
    n_chunks = d // LANES
    n_cls = n_groups * PAIRS_PER_GROUP
    routed = [_route(xs, g, sh, sc, r_hi, r_lo, n_groups, f"{name}_route{si}")
              for si, (xs, sh, sc, _) in enumerate(streams)]
    cnt = jnp.concatenate([r[2][:, 0, :n_cls] for r in routed], axis=0).astype(jnp.int32)
    total = jnp.sum(cnt, axis=0)
    padded = ((total + MOE_TILE - 1) // MOE_TILE) * MOE_TILE
    cls_end = jnp.cumsum(padded)
    base = (cls_end - padded)[None, :] + jnp.cumsum(cnt, axis=0) - cnt
    n_tok = sum(xs.shape[0] * xs.shape[1] for xs, _, _, _ in streams)
    n_tiles = n_tok // MOE_TILE + n_cls
    tile_start = jnp.arange(n_tiles, dtype=jnp.int32) * MOE_TILE
    n_used = (cls_end[-1] // MOE_TILE).astype(jnp.int32)
    last_start = jnp.maximum(cls_end[-1] - MOE_TILE, 0)
    tile_cls = jnp.sum((jnp.minimum(tile_start, last_start)[:, None] >= cls_end[None, :]).astype(jnp.int32), axis=1)
    tile_cls = jnp.minimum(tile_cls, n_cls - 1)
    pairs = [(a, b) for a in range(EXPERTS_PER_GROUP) for b in range(a + 1, EXPERTS_PER_GROUP)]
    lo_tab = jnp.array([gi * EXPERTS_PER_GROUP + a for gi in range(n_groups) for a, _ in pairs], jnp.int32)
    hi_tab = jnp.array([gi * EXPERTS_PER_GROUP + b for gi in range(n_groups) for _, b in pairs], jnp.int32)
    tile_ea = lo_tab[tile_cls]
    tile_eb = hi_tab[tile_cls]

    base_flat = base.reshape(-1).astype(jnp.int32)
    buf = jnp.zeros((n_tiles * MOE_TILE, n_chunks, LANES), F32)
    idx, blk0 = [], 0
    for si, ((xs, _, _, _), (u_flat, meta, c)) in enumerate(zip(streams, routed)):
        nb = c.shape[0]
        tm = xs.shape[1] * xs.shape[0] // nb
        cls_t = meta[..., 0].astype(jnp.int32).reshape(nb, 1, tm)
        rank_t = meta[..., 1].astype(jnp.int32).reshape(nb, 1, tm)
        idx.append((cls_t, rank_t, blk0))
        buf = _dispatch(base_flat, cls_t, rank_t, u_flat, buf, blk0, n_cls, f"{name}_dispatch{si}")
        blk0 += nb
    y = _experts(tile_ea, tile_eb, n_used.reshape(1), buf.reshape(-1, LANES), w1, w3, w2, MOE_TILE,
                 f"{name}_experts")
    y3 = y.reshape(-1, 2 * n_chunks, LANES)
    outs = []
    for si, ((xs, _, _, gate), (_, meta, _), (cls_t, rank_t, b0)) in enumerate(zip(streams, routed, idx)):
        outs.append(_combine(base_flat, cls_t, rank_t, y3, meta, xs, gate, b0, n_cls, f"{name}_combine{si}"))
    return outs


def _rope_tables(n, dqk):
    rows = n // GRID_W
    row = jnp.repeat(jnp.arange(rows), GRID_W).astype(F32)
    col = jnp.tile(jnp.arange(GRID_W), rows).astype(F32)
    n_freq = dqk // 4
    inv = ROPE_BASE ** (-jnp.arange(n_freq, dtype=F32) / n_freq)
    ang = jnp.concatenate([row[:, None] * inv, col[:, None] * inv], axis=-1)
    cos, sin = jnp.cos(ang), jnp.sin(ang)
    reps = LANES // dqk
    cos_l = jnp.tile(jnp.concatenate([cos, cos], axis=-1), (1, reps))
    sin_l = jnp.tile(jnp.concatenate([-sin, sin], axis=-1), (1, reps))
    return cos_l, sin_l


def kernel(x, c, ctx, c_ctx, w_ada, b_ada, norm_g, attn_w_in, attn_q_gain, attn_k_gain, attn_lam, attn_sub_gain, attn_w_out, lru_w_in, lru_conv_w, lru_conv_b, lru_w_a, lru_b_a, lru_w_x, lru_b_x, lru_lam, lru_w_out, pool_w_in, pool_w_grp, pool_scale, moe_router_g, moe_router_e, moe_w1, moe_w3, moe_w2):
    bsz, seq, d = x.shape
    n_ctx = ctx.shape[1]
    depth = w_ada.shape[0]
    dqk = attn_q_gain.shape[-1]
    heads = d // (2 * dqk)
    n_groups = moe_router_g.shape[-1]
    n_experts = moe_router_e.shape[-1]
    lru_w = lru_w_out.shape[1]
    lru_g = lru_w_a.shape[2]

    pad = (-(bsz + 1)) % 8
    cc = jnp.concatenate([c, c_ctx[None, :], jnp.zeros((pad, d), F32)], axis=0)
    mods = _ada(cc, w_ada, b_ada)

    cos_l, sin_l = _rope_tables(seq, dqk)
    cos_c = jnp.ones((n_ctx, LANES), F32)
    sin_c = jnp.zeros((n_ctx, LANES), F32)
    lane = jnp.arange(LANES)
    gmat = (lane[:, None] // dqk == lane[None, :] // dqk).astype(BF16)

    xs, cs = x, ctx
    for i in range(depth):
        need_ctx = i < depth - 1
        mx = [mods[i, :bsz, k * d:(k + 1) * d][:, None, :] for k in range(6)]
        mc = [mods[i, bsz:bsz + 1, k * d:(k + 1) * d][:, None, :] for k in range(6)]
        g1 = norm_g[i, 0][None, :]
        g2 = norm_g[i, 1][None, :]
        kind, j = i % 3, i // 3
        nm = f"l{i}"
        if kind == 0:
            lam_init = 0.8 - 0.6 * math.exp(-0.3 * i)
            lq = attn_lam[j].astype(F32)
            lam = (jnp.exp(jnp.sum(lq[0] * lq[1])) - jnp.exp(jnp.sum(lq[2] * lq[3])) + lam_init).reshape(1)
            w_in = attn_w_in[j].astype(BF16)
            reps = LANES // dqk
            gains = jnp.stack([jnp.tile(attn_q_gain[j], reps) * (dqk ** -0.5 * math.log2(math.e)),
                               jnp.tile(attn_k_gain[j], reps)])[:, None, :]
            sub_gain = attn_sub_gain[j][None, :]
            qkv_x = _qkv_proj(xs, g1, mx[0], mx[1], w_in, gains, cos_l, sin_l, gmat, True, dqk, nm + "_qkv_x")
            qkv_c = _qkv_proj(cs, g1, mc[0], mc[1], w_in, gains, cos_c, sin_c, gmat, False, dqk, nm + "_qkv_c")
            w_out = attn_w_out[j].astype(BF16)
            ox = _attention(lam, qkv_x, [qkv_c, qkv_x], sub_gain, heads, dqk, 1.0 - lam_init, nm + "_attn_x")
            xs = _mm_res(ox, w_out, xs, mx[2], nm + "_out_x")
            if need_ctx:
                oc = _attention(lam, qkv_c, [qkv_c], sub_gain, heads, dqk, 1.0 - lam_init, nm + "_attn_c")
                cs = _mm_res(oc, w_out, cs, mc[2], nm + "_out_c")
        elif kind == 1:
            w_in = lru_w_in[j].astype(BF16)
            w_gate, w_xr = w_in[:, :lru_w], w_in[:, lru_w:]
            conv_b = lru_conv_b[j][None, :]
            ggx, xcx = _lru_proj(xs, g1, mx[0], mx[1], w_gate, w_xr, lru_conv_w[j], conv_b, nm + "_proj_x")
            ggc, xcc = _lru_proj(cs, g1, mc[0], mc[1], w_gate, w_xr, lru_conv_w[j], conv_b, nm + "_proj_c")
            h0 = jnp.zeros((lru_g, bsz, LANES), F32)
            hx_prev, hc_prev = None, None
            for dr, reverse in enumerate((False, True)):
                wab = jnp.concatenate([lru_w_a[j, dr], lru_w_x[j, dr]], axis=-1).astype(BF16)
                bias = jnp.concatenate([lru_b_a[j, dr].reshape(lru_g, 1, LANES),
                                        lru_b_x[j, dr].reshape(lru_g, 1, LANES)], axis=-1)
                lam_d = lru_lam[j, dr].reshape(lru_g, 1, LANES)
                last = dr == 1
                hc_seq, hc_fin = _lru_scan(xcc, wab, bias, lam_d, h0, hc_prev if last else None,
                                           ggc if last else None, reverse, True, f"{nm}_scan_c{dr}")
                hx_seq, _ = _lru_scan(xcx, wab, bias, lam_d, hc_fin, hx_prev if last else None,
                                      ggx if last else None, reverse, False, f"{nm}_scan_x{dr}")
                hx_prev, hc_prev = hx_seq, hc_seq
            w_out = lru_w_out[j].astype(BF16)
            xs = _mm_res(hx_prev, w_out, xs, mx[2], nm + "_out_x")
            if need_ctx:
                cs = _mm_res(hc_prev, w_out, cs, mc[2], nm + "_out_c")
        else:
            w_in = pool_w_in[j].astype(BF16)
            w_grp = pool_w_grp[j].astype(BF16)
            pscale = pool_scale[j][None, :]
            xs_new = _pool_mixer(xs, g1, mx[0], mx[1], w_in, w_grp, pscale, mx[2], nm + "_pool_x")
            if need_ctx:
                cs = _pool_mixer(cs, g1, mc[0], mc[1], w_in, w_grp, pscale, mc[2], nm + "_pool_c")
            xs = xs_new

        r_all = jnp.concatenate([moe_router_g[i], moe_router_e[i],
                                 jnp.zeros((d, LANES - n_groups - n_experts), F32)], axis=-1)
        r_hi = r_all.astype(BF16)
        r_lo = (r_all - r_hi.astype(F32)).astype(BF16)
        w1 = moe_w1[i].astype(BF16)
        w3 = moe_w3[i].astype(BF16)
        w2 = moe_w2[i].astype(BF16)
        streams = [(xs, mx[3], mx[4], mx[5])]
        if need_ctx:
            streams.append((cs, mc[3], mc[4], mc[5]))
        outs = _hier_moe(streams, g2, r_hi, r_lo, w1, w3, w2, n_groups, nm + "_moe")
        xs = outs[0]
        if need_ctx:
            cs = outs[1]
    return xs
```

```python
import functools
import math

import jax
import jax.numpy as jnp
from jax import lax
from jax.experimental import pallas as pl
from jax.experimental.pallas import tpu as pltpu

F32 = jnp.float32
BF16 = jnp.bfloat16

EPS = 1e-6
GRID_W = 64
ROPE_BASE = 10000.0
LRU_C = 8.0
POOL_WINDOWS = (2, 4, 8, 16)
EXPERTS_PER_GROUP = 4
PAIRS_PER_GROUP = EXPERTS_PER_GROUP * (EXPERTS_PER_GROUP - 1) // 2
ROUTE_BLOCK = 512
MOE_TILE = 512
SCAN_ROW_PAD = 8
ATTN_Q_TILE = 256
DMA_QUEUES = 2
LANES = 128
VMEM_LIMIT = 56 << 20
ROW_CHUNK = 512


def _cparams(sem):
    return pltpu.CompilerParams(dimension_semantics=sem, vmem_limit_bytes=VMEM_LIMIT)


def _norm_mod(x, g, shift, scale):
    ms = jnp.mean(x * x, axis=-1, keepdims=True)
    return (x * lax.rsqrt(ms + EPS) * g) * (1.0 + scale) + shift


def _bidx(bm):
    return (lambda b: b) if bm > 1 else (lambda b: 0)


def _ada_kernel(c_ref, w_ref, b_ref, o_ref):
    c = c_ref[...]
    sc = c * jax.nn.sigmoid(c)
    o_ref[0] = jnp.dot(sc, w_ref[0], preferred_element_type=F32,
                       precision=lax.Precision.HIGHEST) + b_ref[0]


def _ada(cc, w_ada, b_ada):
    depth, d, n6 = w_ada.shape
    r = cc.shape[0]
    tn = 1536
    return pl.pallas_call(
        _ada_kernel,
        grid=(depth, n6 // tn),
        in_specs=[pl.BlockSpec((r, d), lambda i, j: (0, 0)),
                  pl.BlockSpec((1, d, tn), lambda i, j: (i, 0, j)),
                  pl.BlockSpec((1, 1, tn), lambda i, j: (i, 0, j))],
        out_specs=pl.BlockSpec((1, r, tn), lambda i, j: (i, 0, j)),
        out_shape=jax.ShapeDtypeStruct((depth, r, n6), F32),
        compiler_params=_cparams(("parallel", "parallel")),
        name="ada",
    )(cc, w_ada, b_ada.reshape(depth, 1, n6))


def _nmm_kernel(*refs, epilogue, n_extra):
    x_ref, g_ref, sh_ref, sc_ref, w_ref = refs[:5]
    extra = refs[5:5 + n_extra]
    outs = refs[5 + n_extra:-1]
    u_ref = refs[-1]
    j = pl.program_id(1)
    n = x_ref.shape[1]

    @pl.when(j == 0)
    def _():
        for r0 in range(0, n, ROW_CHUNK):
            r1 = min(n, r0 + ROW_CHUNK)
            u_ref[r0:r1, :] = _norm_mod(x_ref[0, r0:r1, :], g_ref[...], sh_ref[0], sc_ref[0]).astype(BF16)

    acc = jnp.dot(u_ref[...], w_ref[...], preferred_element_type=F32)
    epilogue(acc, j, extra, outs)


def _nmm(xs, g, shift, scale, w, tn, epilogue, extra, extra_specs, out_shapes, out_specs, name):
    b, n, d = xs.shape
    bi = _bidx(shift.shape[0])
    nout = w.shape[1]
    kern = functools.partial(_nmm_kernel, epilogue=epilogue, n_extra=len(extra))
    return pl.pallas_call(
        kern,
        grid=(b, nout // tn),
        in_specs=[pl.BlockSpec((1, n, d), lambda i, j: (i, 0, 0)),
                  pl.BlockSpec((1, d), lambda i, j: (0, 0)),
                  pl.BlockSpec((1, 1, d), lambda i, j: (bi(i), 0, 0)),
                  pl.BlockSpec((1, 1, d), lambda i, j: (bi(i), 0, 0)),
                  pl.BlockSpec((d, tn), lambda i, j: (0, j))] + list(extra_specs),
        out_specs=out_specs,
        out_shape=out_shapes,
        scratch_shapes=[pltpu.VMEM((n, d), BF16)],
        compiler_params=_cparams(("parallel", "arbitrary")),
        name=name,
    )(xs, g, shift, scale, w, *extra)


def _qkv_epilogue(acc, j, extra, outs, *, use_rope, n_sec, dqk):
    gain_ref, cos_ref, sin_ref, gmat_ref = extra
    o_ref = outs[0]
    tn = acc.shape[1]

    @pl.when(j < 2 * n_sec)
    def _():
        for hb in range(tn // LANES):
            a = acc[:, hb * LANES:(hb + 1) * LANES]
            ss = jnp.dot((a * a).astype(BF16), gmat_ref[...], preferred_element_type=F32)
            y = a * lax.rsqrt(ss * (1.0 / dqk) + EPS) * gain_ref[0]
            if use_rope:
                lane = lax.broadcasted_iota(jnp.int32, y.shape, 1)
                half = dqk // 2
                partner = jnp.where(lane % dqk < half,
                                    pltpu.roll(y, LANES - half, axis=1),
                                    pltpu.roll(y, half, axis=1))
                y = y * cos_ref[...] + partner * sin_ref[...]
            o_ref[0, :, hb * LANES:(hb + 1) * LANES] = y.astype(BF16)

    @pl.when(j >= 2 * n_sec)
    def _():
        o_ref[0] = acc.astype(BF16)


def _qkv_proj(xs, g, shift, scale, w, gains, cos, sin, gmat, use_rope, dqk, name):
    b, n, d = xs.shape
    tn = 256
    n_sec = d // tn
    epi = functools.partial(_qkv_epilogue, use_rope=use_rope, n_sec=n_sec, dqk=dqk)
    extra = (gains, cos, sin, gmat)
    extra_specs = [pl.BlockSpec((1, 1, LANES), lambda i, j: (jnp.minimum(j // n_sec, 1), 0, 0)),
                   pl.BlockSpec((n, LANES), lambda i, j: (0, 0)),
                   pl.BlockSpec((n, LANES), lambda i, j: (0, 0)),
                   pl.BlockSpec((LANES, LANES), lambda i, j: (0, 0))]
    return _nmm(xs, g, shift, scale, w, tn, epi, extra, extra_specs,
                jax.ShapeDtypeStruct((b, n, 3 * d), BF16),
                pl.BlockSpec((1, n, tn), lambda i, j: (i, 0, j)), name)


def _attn_kernel(lam_ref, q_ref, *refs, n_parts, dqk, post_scale):
    k_refs = refs[:n_parts]
    v_refs = refs[n_parts:2 * n_parts]
    sg_ref = refs[2 * n_parts]
    o_ref = refs[2 * n_parts + 1]
    lam = lam_ref[0]
    q = q_ref[0]
    lane = lax.broadcasted_iota(jnp.int32, q.shape, 1)
    zero = jnp.zeros_like(q)
    qm = (jnp.where(lane < dqk, q, zero), jnp.where(lane >= dqk, q, zero))
    dn = (((1,), (1,)), ((), ()))
    es, inv = [], []
    for m in range(2):
        s = [lax.dot_general(qm[m], k_ref[0], dn, preferred_element_type=F32) for k_ref in k_refs]
        mx = functools.reduce(jnp.maximum, [jnp.max(sp, axis=-1, keepdims=True) for sp in s])
        e = [jnp.exp2(sp - mx) for sp in s]
        den = functools.reduce(lambda a, c: a + c, [jnp.sum(ep, axis=-1, keepdims=True) for ep in e])
        es.append(e)
        inv.append(1.0 / den)
    c0 = inv[0]
    c1 = lam * inv[1]
    o = None
    for p in range(n_parts):
        a = (es[0][p] * c0 - es[1][p] * c1).astype(BF16)
        t = jnp.dot(a, v_refs[p][0], preferred_element_type=F32)
        o = t if o is None else o + t
    ms = jnp.mean(o * o, axis=-1, keepdims=True)
    o = o * lax.rsqrt(ms + EPS) * (sg_ref[...] * post_scale)
    o_ref[0] = o.astype(BF16)


def _attention(lam, q_arr, kv_arrs, sub_gain, heads, dqk, post_scale, name):
    b, nq, d3 = q_arr.shape
    d = d3 // 3
    dv = 2 * dqk
    tq = min(nq, ATTN_Q_TILE)
    n_parts = len(kv_arrs)
    kern = functools.partial(_attn_kernel, n_parts=n_parts, dqk=dqk, post_scale=post_scale)
    in_specs = [pl.BlockSpec(memory_space=pltpu.SMEM),
                pl.BlockSpec((1, tq, dv), lambda i, h, t: (i, t, h))]
    in_specs += [pl.BlockSpec((1, a.shape[1], dv), lambda i, h, t: (i, 0, heads + h)) for a in kv_arrs]
    in_specs += [pl.BlockSpec((1, a.shape[1], dv), lambda i, h, t: (i, 0, 2 * heads + h)) for a in kv_arrs]
    in_specs += [pl.BlockSpec((1, dv), lambda i, h, t: (0, 0))]
    return pl.pallas_call(
        kern,
        grid=(b, heads, nq // tq),
        in_specs=in_specs,
        out_specs=pl.BlockSpec((1, tq, dv), lambda i, h, t: (i, t, h)),
        out_shape=jax.ShapeDtypeStruct((b, nq, d), BF16),
        compiler_params=_cparams(("parallel", "parallel", "arbitrary")),
        name=name,
    )(lam, q_arr, *kv_arrs, *kv_arrs, sub_gain)


def _mm_res_kernel(a_ref, w_ref, x_ref, g_ref, o_ref):
    y = jnp.dot(a_ref[0], w_ref[...], preferred_element_type=F32)
    o_ref[0] = x_ref[0] + g_ref[0] * y


def _mm_res(a, w, xs, gate, name):
    b, n, k = a.shape
    d = w.shape[1]
    tm = min(n, 512)
    bi = _bidx(gate.shape[0])
    return pl.pallas_call(
        _mm_res_kernel,
        grid=(b, n // tm),
        in_specs=[pl.BlockSpec((1, tm, k), lambda i, t: (i, t, 0)),
                  pl.BlockSpec((k, d), lambda i, t: (0, 0)),
                  pl.BlockSpec((1, tm, d), lambda i, t: (i, t, 0)),
                  pl.BlockSpec((1, 1, d), lambda i, t: (bi(i), 0, 0))],
        out_specs=pl.BlockSpec((1, tm, d), lambda i, t: (i, t, 0)),
        out_shape=jax.ShapeDtypeStruct((b, n, d), F32),
        compiler_params=_cparams(("parallel", "parallel")),
        name=name,
    )(a, w, xs, gate)


def _gelu_epilogue(acc, j, extra, outs):
    outs[0][0] = jax.nn.gelu(acc).astype(BF16)


def _shift_rows(x, s):
    n = x.shape[0]
    row = lax.broadcasted_iota(jnp.int32, x.shape, 0)
    r = pltpu.roll(x, s % n, axis=0)
    if s >= 0:
        return jnp.where(row >= s, r, 0.0)
    return jnp.where(row < n + s, r, 0.0)


def _conv_epilogue(acc, j, extra, outs):
    cw_ref, cb_ref = extra
    width = cw_ref.shape[0]
    left = width // 2
    y = cb_ref[...] + jnp.zeros_like(acc)
    for k in range(width):
        y = y + cw_ref[k:k + 1, :] * _shift_rows(acc, left - k)
    outs[0][0] = y


def _lru_proj(xs, g, shift, scale, w_gate, w_xr, conv_w, conv_b, name):
    b, n, d = xs.shape
    wd = w_gate.shape[1]
    tn = 256
    gg = _nmm(xs, g, shift, scale, w_gate, tn, _gelu_epilogue, (), [],
              jax.ShapeDtypeStruct((b, n, wd), BF16),
              pl.BlockSpec((1, n, tn), lambda i, j: (i, 0, j)), name + "_gate")
    xc = _nmm(xs, g, shift, scale, w_xr, tn, _conv_epilogue, (conv_w, conv_b),
              [pl.BlockSpec((conv_w.shape[0], tn), lambda i, j: (0, j)),
               pl.BlockSpec((1, tn), lambda i, j: (0, j))],
              jax.ShapeDtypeStruct((b, n, wd), F32),
              pl.BlockSpec((1, n, tn), lambda i, j: (i, 0, j)), name + "_conv")
    return gg, xc


def _lru_scan_kernel(*refs, reverse, reset, final, tt):
    if final:
        xc_ref, wab_ref, bias_ref, lam_ref, h0_ref, prev_ref, gg_ref, o_ref, hfin_ref, a_s, b_s, h_s, st = refs
    else:
        xc_ref, wab_ref, bias_ref, lam_ref, h0_ref, o_ref, hfin_ref, a_s, b_s, h_s, st = refs
    i = pl.program_id(0)
    nb = xc_ref.shape[0]
    ng = wab_ref.shape[0]
    rows = nb * tt
    ts = tt + SCAN_ROW_PAD

    @pl.when(i == 0)
    def _():
        st[...] = h0_ref[...]

    first_t = tt - 1 if reverse else 0
    row = lax.broadcasted_iota(jnp.int32, (rows, LANES), 0)
    for gi in range(ng):
        xg = xc_ref[:, :, gi * LANES:(gi + 1) * LANES].reshape(rows, LANES)
        pre = jnp.dot(xg.astype(BF16), wab_ref[gi], preferred_element_type=F32) + bias_ref[gi]
        r = jax.nn.sigmoid(pre[:, :LANES])
        ig = jax.nn.sigmoid(pre[:, LANES:])
        sp = jax.nn.softplus(-lam_ref[gi])
        log_a = (-LRU_C) * r * sp
        a = jnp.exp(log_a)
        mult = jnp.sqrt(1.0 - a * a)
        if reset:
            mult = jnp.where(jnp.logical_and(row % tt == first_t, i == 0), 1.0, mult)
        bb = mult * ig * xg
        for bi in range(nb):
            a_s[gi, bi * ts:bi * ts + tt, :] = a[bi * tt:(bi + 1) * tt]
            b_s[gi, bi * ts:bi * ts + tt, :] = bb[bi * tt:(bi + 1) * tt]

    def step(s, hs):
        t = (tt - 1 - s) if reverse else s
        out = []
        for gi in range(ng):
            a = a_s[gi, pl.ds(t, nb, stride=ts), :]
            bb = b_s[gi, pl.ds(t, nb, stride=ts), :]
            h = a * hs[gi] + bb
            h_s[gi, pl.ds(t, nb, stride=ts), :] = h
            out.append(h)
        return tuple(out)

    hs = lax.fori_loop(0, tt, step, tuple(st[gi] for gi in range(ng)))
    for gi in range(ng):
        st[gi] = hs[gi]
        hfin_ref[gi] = hs[gi]
        sl = slice(gi * LANES, (gi + 1) * LANES)
        for bi in range(nb):
            hseq = h_s[gi, bi * ts:bi * ts + tt, :]
            if final:
                o_ref[bi, :, sl] = ((prev_ref[bi, :, sl] + hseq) * gg_ref[bi, :, sl].astype(F32)).astype(BF16)
            else:
                o_ref[bi, :, sl] = hseq


def _lru_scan(xc, wab, bias, lam, h0, prev, gg, reverse, reset, name):
    b, n, wd = xc.shape
    ng = wab.shape[0]
    tt = min(n, 32)
    nt = n // tt
    final = prev is not None
    tmap = (lambda i: (0, nt - 1 - i, 0)) if reverse else (lambda i: (0, i, 0))
    full3 = lambda i: (0, 0, 0)
    seq_spec = pl.BlockSpec((b, tt, wd), tmap)
    in_specs = [seq_spec,
                pl.BlockSpec(wab.shape, full3),
                pl.BlockSpec(bias.shape, full3),
                pl.BlockSpec(lam.shape, full3),
                pl.BlockSpec(h0.shape, full3)]
    args = [xc, wab, bias, lam, h0]
    if final:
        in_specs += [seq_spec, seq_spec]
        args += [prev, gg]
    kern = functools.partial(_lru_scan_kernel, reverse=reverse, reset=reset, final=final, tt=tt)
    return pl.pallas_call(
        kern,
        grid=(nt,),
        in_specs=in_specs,
        out_specs=[seq_spec, pl.BlockSpec(h0.shape, full3)],
        out_shape=[jax.ShapeDtypeStruct((b, n, wd), BF16 if final else F32),
                   jax.ShapeDtypeStruct(h0.shape, F32)],
        scratch_shapes=[pltpu.VMEM((ng, b * (tt + SCAN_ROW_PAD), LANES), F32),
                        pltpu.VMEM((ng, b * (tt + SCAN_ROW_PAD), LANES), F32),
                        pltpu.VMEM((ng, b * (tt + SCAN_ROW_PAD), LANES), F32),
                        pltpu.VMEM(h0.shape, F32)],
        compiler_params=_cparams(("arbitrary",)),
        name=name,
    )(*args)


def _window_mean_minus_self(x, window):
    n = x.shape[0]
    back, fwd = x, x
    span = 1
    while span < window // 2:
        back = back + _shift_rows(back, span)
        fwd = fwd + _shift_rows(fwd, -span)
        span *= 2
    p = _shift_rows(back, 1) + fwd
    t = lax.broadcasted_iota(jnp.int32, x.shape, 0)
    lo = jnp.maximum(t - window // 2, 0)
    hi = jnp.minimum(t + window // 2, n)
    cnt = (hi - lo).astype(F32)
    return p / cnt - x


def _pool_epilogue(acc, j, extra, outs):
    wg_ref, ps_ref, xcol_ref, gate_ref = extra
    o_ref = outs[0]
    for gidx, window in enumerate(POOL_WINDOWS):
        @pl.when(j == gidx)
        def _(window=window):
            pooled = _window_mean_minus_self(acc, window).astype(BF16)
            y = jnp.dot(pooled, wg_ref[0], preferred_element_type=F32) * ps_ref[...]
            o_ref[0] = xcol_ref[0] + gate_ref[0] * y


def _pool_mixer(xs, g, shift, scale, w_in, w_grp, pscale, gate, name):
    b, n, d = xs.shape
    ngrp, gd, _ = w_grp.shape
    bi = _bidx(gate.shape[0])
    extra = (w_grp, pscale, xs, gate)
    extra_specs = [pl.BlockSpec((1, gd, gd), lambda i, j: (j, 0, 0)),
                   pl.BlockSpec((1, gd), lambda i, j: (0, j)),
                   pl.BlockSpec((1, n, gd), lambda i, j: (i, 0, j)),
                   pl.BlockSpec((1, 1, gd), lambda i, j: (bi(i), 0, j))]
    return _nmm(xs, g, shift, scale, w_in, gd, _pool_epilogue, extra, extra_specs,
                jax.ShapeDtypeStruct((b, n, d), F32),
                pl.BlockSpec((1, n, gd), lambda i, j: (i, 0, j)), name)


def _route_kernel(x_ref, g_ref, sh_ref, sc_ref, rhi_ref, rlo_ref, u_ref, meta_ref, cnt_ref, *, n_groups):
    u = _norm_mod(x_ref[0], g_ref[...], sh_ref[0], sc_ref[0])
    tm = u.shape[0]
    n_chunks = u.shape[1] // LANES
    for jc in range(n_chunks):
        u_ref[pl.ds(jc, tm, stride=n_chunks), :] = u[:, jc * LANES:(jc + 1) * LANES]
    u_hi = u.astype(BF16)
    u_lo = (u - u_hi.astype(F32)).astype(BF16)
    lg = (jnp.dot(u_hi, rhi_ref[...], preferred_element_type=F32)
          + jnp.dot(u_lo, rhi_ref[...], preferred_element_type=F32)
          + jnp.dot(u_hi, rlo_ref[...], preferred_element_type=F32))
    lane = lax.broadcasted_iota(jnp.int32, lg.shape, 1)
    neg = jnp.float32(-jnp.inf)
    big = jnp.int32(1 << 20)
    is_g = lane < n_groups
    gmax = jnp.max(jnp.where(is_g, lg, neg), axis=-1, keepdims=True)
    g_sel = jnp.min(jnp.where(jnp.logical_and(is_g, lg == gmax), lane, big), axis=-1, keepdims=True)
    p_sel = 1.0 / jnp.sum(jnp.where(is_g, jnp.exp(lg - gmax), 0.0), axis=-1, keepdims=True)
    e_idx = lane - n_groups
    in_grp = jnp.logical_and(e_idx >= g_sel * EXPERTS_PER_GROUP, e_idx < (g_sel + 1) * EXPERTS_PER_GROUP)
    m1 = jnp.max(jnp.where(in_grp, lg, neg), axis=-1, keepdims=True)
    i1 = jnp.min(jnp.where(jnp.logical_and(in_grp, lg == m1), lane, big), axis=-1, keepdims=True)
    rest = jnp.logical_and(in_grp, lane != i1)
    m2 = jnp.max(jnp.where(rest, lg, neg), axis=-1, keepdims=True)
    i2 = jnp.min(jnp.where(jnp.logical_and(rest, lg == m2), lane, big), axis=-1, keepdims=True)
    e2 = jnp.exp(m2 - m1)
    w1 = p_sel / (1.0 + e2)
    w2 = w1 * e2
    off = n_groups + g_sel * EXPERTS_PER_GROUP
    la = jnp.minimum(i1, i2) - off
    lb = jnp.maximum(i1, i2) - off
    pair = ((la * (2 * EXPERTS_PER_GROUP - 1 - la)) >> 1) + lb - la - 1
    cls = g_sel * PAIRS_PER_GROUP + pair
    c_lo = jnp.where(i1 < i2, w1, w2)
    c_hi = jnp.where(i1 < i2, w2, w1)
    onehot = lane == cls
    row = lax.broadcasted_iota(jnp.int32, (tm, tm), 0)
    col = lax.broadcasted_iota(jnp.int32, (tm, tm), 1)
    tri = jnp.where(row > col, 1.0, 0.0).astype(BF16)
    before = jnp.dot(tri, jnp.where(onehot, 1.0, 0.0).astype(BF16), preferred_element_type=F32)
    rank = jnp.sum(jnp.where(onehot, before, 0.0), axis=-1, keepdims=True)
    meta_ref[0] = jnp.where(lane == 0, cls.astype(F32),
                            jnp.where(lane == 1, rank,
                                      jnp.where(lane == 2, c_lo, jnp.where(lane == 3, c_hi, 0.0))))
    cnt = jnp.sum(jnp.where(onehot, 1.0, 0.0), axis=0, keepdims=True)
    cnt_ref[0] = jnp.broadcast_to(cnt, cnt_ref.shape[1:])


def _route(xs, g, shift, scale, r_hi, r_lo, n_groups, name):
    b, n, d = xs.shape
    tm = min(n, ROUTE_BLOCK)
    n_chunks = d // LANES
    nt = n // tm
    bi = _bidx(shift.shape[0])
    kern = functools.partial(_route_kernel, n_groups=n_groups)
    return pl.pallas_call(
        kern,
        grid=(b, nt),
        in_specs=[pl.BlockSpec((1, tm, d), lambda i, t: (i, t, 0)),
                  pl.BlockSpec((1, d), lambda i, t: (0, 0)),
                  pl.BlockSpec((1, 1, d), lambda i, t: (bi(i), 0, 0)),
                  pl.BlockSpec((1, 1, d), lambda i, t: (bi(i), 0, 0)),
                  pl.BlockSpec((d, LANES), lambda i, t: (0, 0)),
                  pl.BlockSpec((d, LANES), lambda i, t: (0, 0))],
        out_specs=[pl.BlockSpec((tm * n_chunks, LANES), lambda i, t: (i * nt + t, 0)),
                   pl.BlockSpec((1, tm, LANES), lambda i, t: (i, t, 0)),
                   pl.BlockSpec((1, 8, LANES), lambda i, t: (i * nt + t, 0, 0))],
        out_shape=[jax.ShapeDtypeStruct((b * n * n_chunks, LANES), F32),
                   jax.ShapeDtypeStruct((b, n, LANES), F32),
                   jax.ShapeDtypeStruct((b * nt, 8, LANES), F32)],
        compiler_params=_cparams(("parallel", "parallel")),
        name=name,
    )(xs, g, shift, scale, r_hi, r_lo)


def _burst_dma(make, n, wait_all):
    def start(i, c):
        for pr in range(2):
            make(2 * i + pr, pr).start(priority=pr)
        return c

    lax.fori_loop(0, n // 2, start, 0)
    for pr in range(2):
        wait_all(pr)


def _dispatch_kernel(base_ref, cls_ref, rank_ref, u_ref, buf_in_ref, buf_ref, sems, *, blk0, n_cls):
    del buf_in_ref
    tb = cls_ref.shape[-1]
    rows = buf_ref.shape[1]
    boff = (blk0 + pl.program_id(0)) * n_cls

    def make(k, slot):
        pos = base_ref[boff + cls_ref[0, 0, k]] + rank_ref[0, 0, k]
        src = u_ref.at[pl.ds(pl.multiple_of(k * rows, rows), rows)]
        return pltpu.make_async_copy(src, buf_ref.at[pos], sems.at[slot])

    def wait_all(slot):
        half = buf_ref.at[pl.ds(0, tb // 2)]
        pltpu.make_async_copy(half, half, sems.at[slot]).wait()

    _burst_dma(make, tb, wait_all)


def _dispatch(base, cls, rank, u_flat, buf, blk0, n_cls, name):
    nb, _, tb = cls.shape
    rows = buf.shape[1]
    smem = lambda: pl.BlockSpec((1, 1, tb), lambda i, bs: (i, 0, 0), memory_space=pltpu.SMEM)
    grid_spec = pltpu.PrefetchScalarGridSpec(
        num_scalar_prefetch=1,
        grid=(nb,),
        in_specs=[smem(), smem(),
                  pl.BlockSpec((tb * rows, LANES), lambda i, bs: (i, 0)),
                  pl.BlockSpec(memory_space=pl.ANY)],
        out_specs=pl.BlockSpec(memory_space=pl.ANY),
        scratch_shapes=[pltpu.SemaphoreType.DMA((DMA_QUEUES,))],
    )
    return pl.pallas_call(
        functools.partial(_dispatch_kernel, blk0=blk0, n_cls=n_cls),
        grid_spec=grid_spec,
        out_shape=jax.ShapeDtypeStruct(buf.shape, buf.dtype),
        input_output_aliases={4: 0},
        compiler_params=pltpu.CompilerParams(dimension_semantics=("arbitrary",), has_side_effects=True),
        name=name,
    )(base, cls, rank, u_flat, buf)


def _experts_kernel(ea_ref, eb_ref, nused_ref, x_ref, w1a_ref, w3a_ref, w2a_ref, w1b_ref, w3b_ref, w2b_ref,
                    y_ref):
    del ea_ref, eb_ref
    k = pl.program_id(0)
    n_chunks = w1a_ref.shape[1] // LANES
    mt = x_ref.shape[0] // n_chunks

    @pl.when(k < nused_ref[0])
    def _():
        x = jnp.concatenate([x_ref[pl.ds(jc, mt, stride=n_chunks), :] for jc in range(n_chunks)],
                            axis=-1).astype(BF16)
        for half, (w1_ref, w3_ref, w2_ref) in enumerate(((w1a_ref, w3a_ref, w2a_ref),
                                                         (w1b_ref, w3b_ref, w2b_ref))):
            h1 = jnp.dot(x, w1_ref[0], preferred_element_type=F32)
            h3 = jnp.dot(x, w3_ref[0], preferred_element_type=F32)
            he = (h1 * jax.nn.sigmoid(h1) * h3).astype(BF16)
            y = jnp.dot(he, w2_ref[0], preferred_element_type=F32)
            for jc in range(n_chunks):
                y_ref[pl.ds(half * n_chunks + jc, mt, stride=2 * n_chunks), :] = y[:, jc * LANES:(jc + 1) * LANES]

    @pl.when(k >= nused_ref[0])
    def _():
        y_ref[...] = jnp.zeros_like(y_ref)


def _experts(tile_ea, tile_eb, n_used, xbuf, w1, w3, w2, mt, name):
    ne, d, f = w1.shape
    n_chunks = d // LANES
    n_tiles = xbuf.shape[0] // (mt * n_chunks)
    wa = lambda k, ea, eb, nu: (ea[k], 0, 0)
    wb = lambda k, ea, eb, nu: (eb[k], 0, 0)
    grid_spec = pltpu.PrefetchScalarGridSpec(
        num_scalar_prefetch=3,
        grid=(n_tiles,),
        in_specs=[pl.BlockSpec((mt * n_chunks, LANES), lambda k, ea, eb, nu: (k, 0)),
                  pl.BlockSpec((1, d, f), wa), pl.BlockSpec((1, d, f), wa), pl.BlockSpec((1, f, d), wa),
                  pl.BlockSpec((1, d, f), wb), pl.BlockSpec((1, d, f), wb), pl.BlockSpec((1, f, d), wb)],
        out_specs=pl.BlockSpec((mt * 2 * n_chunks, LANES), lambda k, ea, eb, nu: (k, 0)),
    )
    return pl.pallas_call(
        _experts_kernel,
        grid_spec=grid_spec,
        out_shape=jax.ShapeDtypeStruct((n_tiles * mt * 2 * n_chunks, LANES), F32),
        compiler_params=_cparams(("arbitrary",)),
        name=name,
    )(tile_ea, tile_eb, n_used, xbuf, w1, w3, w2, w1, w3, w2)


def _combine_kernel(base_ref, cls_ref, rank_ref, y_hbm, meta_ref, x_ref, g_ref, o_ref, buf, sems, *, blk0, n_cls):
    tb = cls_ref.shape[-1]
    rows = y_hbm.shape[1]
    n_chunks = rows // 2
    boff = (blk0 + pl.program_id(0) * pl.num_programs(1) + pl.program_id(1)) * n_cls

    def make(k, slot):
        pos = base_ref[boff + cls_ref[0, 0, k]] + rank_ref[0, 0, k]
        dst = buf.at[pl.ds(pl.multiple_of(k * rows, rows), rows)]
        return pltpu.make_async_copy(y_hbm.at[pos], dst, sems.at[slot])

    def wait_all(slot):
        half = buf.at[pl.ds(0, (tb // 2) * rows)]
        pltpu.make_async_copy(half, half, sems.at[slot]).wait()

    _burst_dma(make, tb, wait_all)
    meta = meta_ref[0]
    c_lo = meta[:, 2:3]
    c_hi = meta[:, 3:4]
    for jc in range(n_chunks):
        sl = slice(jc * LANES, (jc + 1) * LANES)
        y_lo = buf[pl.ds(jc, tb, stride=rows), :]
        y_hi = buf[pl.ds(n_chunks + jc, tb, stride=rows), :]
        o_ref[0, :, sl] = x_ref[0, :, sl] + g_ref[0, :, sl] * (c_lo * y_lo + c_hi * y_hi)


def _combine(base, cls, rank, y3, meta, xs, gate, blk0, n_cls, name):
    b, n, d = xs.shape
    tb = cls.shape[-1]
    nt = n // tb
    rows = y3.shape[1]
    bi = _bidx(gate.shape[0])
    smem = lambda: pl.BlockSpec((1, 1, tb), lambda i, t, bs: (i * nt + t, 0, 0), memory_space=pltpu.SMEM)
    grid_spec = pltpu.PrefetchScalarGridSpec(
        num_scalar_prefetch=1,
        grid=(b, nt),
        in_specs=[smem(), smem(),
                  pl.BlockSpec(memory_space=pl.ANY),
                  pl.BlockSpec((1, tb, LANES), lambda i, t, bs: (i, t, 0)),
                  pl.BlockSpec((1, tb, d), lambda i, t, bs: (i, t, 0)),
                  pl.BlockSpec((1, 1, d), lambda i, t, bs: (bi(i), 0, 0))],
        out_specs=pl.BlockSpec((1, tb, d), lambda i, t, bs: (i, t, 0)),
        scratch_shapes=[pltpu.VMEM((tb * rows, LANES), F32),
                        pltpu.SemaphoreType.DMA((DMA_QUEUES,))],
    )
    return pl.pallas_call(
        functools.partial(_combine_kernel, blk0=blk0, n_cls=n_cls),
        grid_spec=grid_spec,
        out_shape=jax.ShapeDtypeStruct((b, n, d), F32),
        compiler_params=_cparams(("arbitrary", "arbitrary")),
        name=name,
    )(base, cls, rank, y3, meta, xs, gate)


def _hier_moe(streams, g, r_hi, r_lo, w1, w3, w2, n_groups, name):
    d = w1.shape[1]
    n_chunks = d // LANES
    n_cls = n_groups * PAIRS_PER_GROUP
    routed = [_route(xs, g, sh, sc, r_hi, r_lo, n_groups, f"{name}_route{si}")
              for si, (xs, sh, sc, _) in enumerate(streams)]
    cnt = jnp.concatenate([r[2][:, 0, :n_cls] for r in routed], axis=0).astype(jnp.int32)
    total = jnp.sum(cnt, axis=0)
    padded = ((total + MOE_TILE - 1) // MOE_TILE) * MOE_TILE
    cls_end = jnp.cumsum(padded)
    base = (cls_end - padded)[None, :] + jnp.cumsum(cnt, axis=0) - cnt
    n_tok = sum(xs.shape[0] * xs.shape[1] for xs, _, _, _ in streams)
    n_tiles = n_tok // MOE_TILE + n_cls
    tile_start = jnp.arange(n_tiles, dtype=jnp.int32) * MOE_TILE
    n_used = (cls_end[-1] // MOE_TILE).astype(jnp.int32)
    last_start = jnp.maximum(cls_end[-1] - MOE_TILE, 0)
    tile_cls = jnp.sum((jnp.minimum(tile_start, last_start)[:, None] >= cls_end[None, :]).astype(jnp.int32), axis=1)
    tile_cls = jnp.minimum(tile_cls, n_cls - 1)
    pairs = [(a, b) for a in range(EXPERTS_PER_GROUP) for b in range(a + 1, EXPERTS_PER_GROUP)]
    lo_tab = jnp.array([gi * EXPERTS_PER_GROUP + a for gi in range(n_groups) for a, _ in pairs], jnp.int32)
    hi_tab = jnp.array([gi * EXPERTS_PER_GROUP + b for gi in range(n_groups) for _, b in pairs], jnp.int32)
    tile_ea = lo_tab[tile_cls]
    tile_eb = hi_tab[tile_cls]

    base_flat = base.reshape(-1).astype(jnp.int32)
    buf = jnp.zeros((n_tiles * MOE_TILE, n_chunks, LANES), F32)
    idx, blk0 = [], 0
    for si, ((xs, _, _, _), (u_flat, meta, c)) in enumerate(zip(streams, routed)):
        nb = c.shape[0]
        tm = xs.shape[1] * xs.shape[0] // nb
        cls_t = meta[..., 0].astype(jnp.int32).reshape(nb, 1, tm)
        rank_t = meta[..., 1].astype(jnp.int32).reshape(nb, 1, tm)
        idx.append((cls_t, rank_t, blk0))
        buf = _dispatch(base_flat, cls_t, rank_t, u_flat, buf, blk0, n_cls, f"{name}_dispatch{si}")
        blk0 += nb
    y = _experts(tile_ea, tile_eb, n_used.reshape(1), buf.reshape(-1, LANES), w1, w3, w2, MOE_TILE,
                 f"{name}_experts")
    y3 = y.reshape(-1, 2 * n_chunks, LANES)
    outs = []
    for si, ((xs, _, _, gate), (_, meta, _), (cls_t, rank_t, b0)) in enumerate(zip(streams, routed, idx)):
        outs.append(_combine(base_flat, cls_t, rank_t, y3, meta, xs, gate, b0, n_cls, f"{name}_combine{si}"))
    return outs


def _rope_tables(n, dqk):
    rows = n // GRID_W
    row = jnp.repeat(jnp.arange(rows), GRID_W).astype(F32)
    col = jnp.tile(jnp.arange(GRID_W), rows).astype(F32)
    n_freq = dqk // 4
    inv = ROPE_BASE ** (-jnp.arange(n_freq, dtype=F32) / n_freq)
    ang = jnp.concatenate([row[:, None] * inv, col[:, None] * inv], axis=-1)
    cos, sin = jnp.cos(ang), jnp.sin(ang)
    reps = LANES // dqk
    cos_l = jnp.tile(jnp.concatenate([cos, cos], axis=-1), (1, reps))
    sin_l = jnp.tile(jnp.concatenate([-sin, sin], axis=-1), (1, reps))
    return cos_l, sin_l


def kernel(x, c, ctx, c_ctx, w_ada, b_ada, norm_g, attn_w_in, attn_q_gain, attn_k_gain, attn_lam, attn_sub_gain, attn_w_out, lru_w_in, lru_conv_w, lru_conv_b, lru_w_a, lru_b_a, lru_w_x, lru_b_x, lru_lam, lru_w_out, pool_w_in, pool_w_grp, pool_scale, moe_router_g, moe_router_e, moe_w1, moe_w3, moe_w2):
    bsz, seq, d = x.shape
    n_ctx = ctx.shape[1]
    depth = w_ada.shape[0]
    dqk = attn_q_gain.shape[-1]
    heads = d // (2 * dqk)
    n_groups = moe_router_g.shape[-1]
    n_experts = moe_router_e.shape[-1]
    lru_w = lru_w_out.shape[1]
    lru_g = lru_w_a.shape[2]

    pad = (-(bsz + 1)) % 8
    cc = jnp.concatenate([c, c_ctx[None, :], jnp.zeros((pad, d), F32)], axis=0)
    mods = _ada(cc, w_ada, b_ada)

    cos_l, sin_l = _rope_tables(seq, dqk)
    cos_c = jnp.ones((n_ctx, LANES), F32)
    sin_c = jnp.zeros((n_ctx, LANES), F32)
    lane = jnp.arange(LANES)
    gmat = (lane[:, None] // dqk == lane[None, :] // dqk).astype(BF16)

    xs, cs = x, ctx
    for i in range(depth):
        need_ctx = i < depth - 1
        mx = [mods[i, :bsz, k * d:(k + 1) * d][:, None, :] for k in range(6)]
        mc = [mods[i, bsz:bsz + 1, k * d:(k + 1) * d][:, None, :] for k in range(6)]
        g1 = norm_g[i, 0][None, :]
        g2 = norm_g[i, 1][None, :]
        kind, j = i % 3, i // 3
        nm = f"l{i}"
        if kind == 0:
            lam_init = 0.8 - 0.6 * math.exp(-0.3 * i)
            lq = attn_lam[j].astype(F32)
            lam = (jnp.exp(jnp.sum(lq[0] * lq[1])) - jnp.exp(jnp.sum(lq[2] * lq[3])) + lam_init).reshape(1)
            w_in = attn_w_in[j].astype(BF16)
            reps = LANES // dqk
            gains = jnp.stack([jnp.tile(attn_q_gain[j], reps) * (dqk ** -0.5 * math.log2(math.e)),
                               jnp.tile(attn_k_gain[j], reps)])[:, None, :]
            sub_gain = attn_sub_gain[j][None, :]
            qkv_x = _qkv_proj(xs, g1, mx[0], mx[1], w_in, gains, cos_l, sin_l, gmat, True, dqk, nm + "_qkv_x")
            qkv_c = _qkv_proj(cs, g1, mc[0], mc[1], w_in, gains, cos_c, sin_c, gmat, False, dqk, nm + "_qkv_c")
            w_out = attn_w_out[j].astype(BF16)
            ox = _attention(lam, qkv_x, [qkv_c, qkv_x], sub_gain, heads, dqk, 1.0 - lam_init, nm + "_attn_x")
            xs = _mm_res(ox, w_out, xs, mx[2], nm + "_out_x")
            if need_ctx:
                oc = _attention(lam, qkv_c, [qkv_c], sub_gain, heads, dqk, 1.0 - lam_init, nm + "_attn_c")
                cs = _mm_res(oc, w_out, cs, mc[2], nm + "_out_c")
        elif kind == 1:
            w_in = lru_w_in[j].astype(BF16)
            w_gate, w_xr = w_in[:, :lru_w], w_in[:, lru_w:]
            conv_b = lru_conv_b[j][None, :]
            ggx, xcx = _lru_proj(xs, g1, mx[0], mx[1], w_gate, w_xr, lru_conv_w[j], conv_b, nm + "_proj_x")
            ggc, xcc = _lru_proj(cs, g1, mc[0], mc[1], w_gate, w_xr, lru_conv_w[j], conv_b, nm + "_proj_c")
            h0 = jnp.zeros((lru_g, bsz, LANES), F32)
            hx_prev, hc_prev = None, None
            for dr, reverse in enumerate((False, True)):
                wab = jnp.concatenate([lru_w_a[j, dr], lru_w_x[j, dr]], axis=-1).astype(BF16)
                bias = jnp.concatenate([lru_b_a[j, dr].reshape(lru_g, 1, LANES),
                                        lru_b_x[j, dr].reshape(lru_g, 1, LANES)], axis=-1)
                lam_d = lru_lam[j, dr].reshape(lru_g, 1, LANES)
                last = dr == 1
                hc_seq, hc_fin = _lru_scan(xcc, wab, bias, lam_d, h0, hc_prev if last else None,
                                           ggc if last else None, reverse, True, f"{nm}_scan_c{dr}")
                hx_seq, _ = _lru_scan(xcx, wab, bias, lam_d, hc_fin, hx_prev if last else None,
                                      ggx if last else None, reverse, False, f"{nm}_scan_x{dr}")
                hx_prev, hc_prev = hx_seq, hc_seq
            w_out = lru_w_out[j].astype(BF16)
            xs = _mm_res(hx_prev, w_out, xs, mx[2], nm + "_out_x")
            if need_ctx:
                cs = _mm_res(hc_prev, w_out, cs, mc[2], nm + "_out_c")
        else:
            w_in = pool_w_in[j].astype(BF16)
            w_grp = pool_w_grp[j].astype(BF16)
            pscale = pool_scale[j][None, :]
            xs_new = _pool_mixer(xs, g1, mx[0], mx[1], w_in, w_grp, pscale, mx[2], nm + "_pool_x")
            if need_ctx:
                cs = _pool_mixer(cs, g1, mc[0], mc[1], w_in, w_grp, pscale, mc[2], nm + "_pool_c")
            xs = xs_new

        r_all = jnp.concatenate([moe_router_g[i], moe_router_e[i],
                                 jnp.zeros((d, LANES - n_groups - n_experts), F32)], axis=-1)
        r_hi = r_all.astype(BF16)
        r_lo = (r_all - r_hi.astype(F32)).astype(BF16)
        w1 = moe_w1[i].astype(BF16)
        w3 = moe_w3[i].astype(BF16)
        w2 = moe_w2[i].astype(BF16)
        streams = [(xs, mx[3], mx[4], mx[5])]
        if need_ctx:
            streams.append((cs, mc[3], mc[4], mc[5]))
        outs = _hier_moe(streams, g2, r_hi, r_lo, w1, w3, w2, n_groups, nm + "_moe")
        xs = outs[0]
        if need_ctx:
            cs = outs[1]
    return xs
```

```python
import functools
import math

import jax
import jax.numpy as jnp
from jax import lax
from jax.experimental import pallas as pl
from jax.experimental.pallas import tpu as pltpu

F32 = jnp.float32
BF16 = jnp.bfloat16

EPS = 1e-6
GRID_W = 64
ROPE_BASE = 10000.0
LRU_C = 8.0
POOL_WINDOWS = (2, 4, 8, 16)
EXPERTS_PER_GROUP = 4
PAIRS_PER_GROUP = EXPERTS_PER_GROUP * (EXPERTS_PER_GROUP - 1) // 2
ROUTE_BLOCK = 512
MOE_TILE = 512
SCAN_ROW_PAD = 8
ATTN_Q_TILE = 256
ATTN_KEY_CHUNK = 512
DMA_QUEUES = 2
LANES = 128
VMEM_LIMIT = 56 << 20
ROW_CHUNK = 512


def _cparams(sem):
    return pltpu.CompilerParams(dimension_semantics=sem, vmem_limit_bytes=VMEM_LIMIT)


def _norm_mod(x, g, shift, scale):
    ms = jnp.mean(x * x, axis=-1, keepdims=True)
    return (x * lax.rsqrt(ms + EPS) * g) * (1.0 + scale) + shift


def _bidx(bm):
    return (lambda b: b) if bm > 1 else (lambda b: 0)


def _ada_kernel(c_ref, w_ref, b_ref, o_ref):
    c = c_ref[...]
    sc = c * jax.nn.sigmoid(c)
    o_ref[0] = jnp.dot(sc, w_ref[0], preferred_element_type=F32,
                       precision=lax.Precision.HIGHEST) + b_ref[0]


def _ada(cc, w_ada, b_ada):
    depth, d, n6 = w_ada.shape
    r = cc.shape[0]
    tn = 1536
    return pl.pallas_call(
        _ada_kernel,
        grid=(depth, n6 // tn),
        in_specs=[pl.BlockSpec((r, d), lambda i, j: (0, 0)),
                  pl.BlockSpec((1, d, tn), lambda i, j: (i, 0, j)),
                  pl.BlockSpec((1, 1, tn), lambda i, j: (i, 0, j))],
        out_specs=pl.BlockSpec((1, r, tn), lambda i, j: (i, 0, j)),
        out_shape=jax.ShapeDtypeStruct((depth, r, n6), F32),
        compiler_params=_cparams(("parallel", "parallel")),
        name="ada",
    )(cc, w_ada, b_ada.reshape(depth, 1, n6))


def _nmm_kernel(*refs, epilogue, n_extra):
    x_ref, g_ref, sh_ref, sc_ref, w_ref = refs[:5]
    extra = refs[5:5 + n_extra]
    outs = refs[5 + n_extra:-1]
    u_ref = refs[-1]
    j = pl.program_id(1)
    n = x_ref.shape[1]

    @pl.when(j == 0)
    def _():
        for r0 in range(0, n, ROW_CHUNK):
            r1 = min(n, r0 + ROW_CHUNK)
            u_ref[r0:r1, :] = _norm_mod(x_ref[0, r0:r1, :], g_ref[...], sh_ref[0], sc_ref[0]).astype(BF16)

    acc = jnp.dot(u_ref[...], w_ref[...], preferred_element_type=F32)
    epilogue(acc, j, extra, outs)


def _nmm(xs, g, shift, scale, w, tn, epilogue, extra, extra_specs, out_shapes, out_specs, name):
    b, n, d = xs.shape
    bi = _bidx(shift.shape[0])
    nout = w.shape[1]
    kern = functools.partial(_nmm_kernel, epilogue=epilogue, n_extra=len(extra))
    return pl.pallas_call(
        kern,
        grid=(b, nout // tn),
        in_specs=[pl.BlockSpec((1, n, d), lambda i, j: (i, 0, 0)),
                  pl.BlockSpec((1, d), lambda i, j: (0, 0)),
                  pl.BlockSpec((1, 1, d), lambda i, j: (bi(i), 0, 0)),
                  pl.BlockSpec((1, 1, d), lambda i, j: (bi(i), 0, 0)),
                  pl.BlockSpec((d, tn), lambda i, j: (0, j))] + list(extra_specs),
        out_specs=out_specs,
        out_shape=out_shapes,
        scratch_shapes=[pltpu.VMEM((n, d), BF16)],
        compiler_params=_cparams(("parallel", "arbitrary")),
        name=name,
    )(xs, g, shift, scale, w, *extra)


def _qkv_epilogue(acc, j, extra, outs, *, use_rope, n_sec, dqk):
    gain_ref, cos_ref, sin_ref, gmat_ref = extra
    o_ref = outs[0]
    tn = acc.shape[1]

    @pl.when(j < 2 * n_sec)
    def _():
        for hb in range(tn // LANES):
            a = acc[:, hb * LANES:(hb + 1) * LANES]
            ss = jnp.dot((a * a).astype(BF16), gmat_ref[...], preferred_element_type=F32)
            y = a * lax.rsqrt(ss * (1.0 / dqk) + EPS) * gain_ref[0]
            if use_rope:
                lane = lax.broadcasted_iota(jnp.int32, y.shape, 1)
                half = dqk // 2
                partner = jnp.where(lane % dqk < half,
                                    pltpu.roll(y, LANES - half, axis=1),
                                    pltpu.roll(y, half, axis=1))
                y = y * cos_ref[...] + partner * sin_ref[...]
            o_ref[0, :, hb * LANES:(hb + 1) * LANES] = y.astype(BF16)

    @pl.when(j >= 2 * n_sec)
    def _():
        o_ref[0] = acc.astype(BF16)


def _qkv_proj(xs, g, shift, scale, w, gains, cos, sin, gmat, use_rope, dqk, name):
    b, n, d = xs.shape
    tn = 256
    n_sec = d // tn
    epi = functools.partial(_qkv_epilogue, use_rope=use_rope, n_sec=n_sec, dqk=dqk)
    extra = (gains, cos, sin, gmat)
    extra_specs = [pl.BlockSpec((1, 1, LANES), lambda i, j: (jnp.minimum(j // n_sec, 1), 0, 0)),
                   pl.BlockSpec((n, LANES), lambda i, j: (0, 0)),
                   pl.BlockSpec((n, LANES), lambda i, j: (0, 0)),
                   pl.BlockSpec((LANES, LANES), lambda i, j: (0, 0))]
    return _nmm(xs, g, shift, scale, w, tn, epi, extra, extra_specs,
                jax.ShapeDtypeStruct((b, n, 3 * d), BF16),
                pl.BlockSpec((1, n, tn), lambda i, j: (i, 0, j)), name)


def _attn_kernel(lam_ref, q_ref, *refs, n_parts, dqk, post_scale):
    k_refs = refs[:n_parts]
    v_refs = refs[n_parts:2 * n_parts]
    sg_ref = refs[2 * n_parts]
    o_ref = refs[2 * n_parts + 1]
    lam = lam_ref[0]
    q = q_ref[0]
    lane = lax.broadcasted_iota(jnp.int32, q.shape, 1)
    zero = jnp.zeros_like(q)
    qm = (jnp.where(lane < dqk, q, zero), jnp.where(lane >= dqk, q, zero))
    dn = (((1,), (1,)), ((), ()))
    chunks = []
    for k_ref, v_ref in zip(k_refs, v_refs):
        nk = k_ref.shape[1]
        for c0 in range(0, nk, ATTN_KEY_CHUNK):
            chunks.append((k_ref, v_ref, c0, min(nk, c0 + ATTN_KEY_CHUNK)))
    tq = q.shape[0]
    outs = []
    for m in range(2):
        mx = jnp.full((tq, 1), -jnp.inf, F32)
        den = jnp.zeros((tq, 1), F32)
        acc = jnp.zeros((tq, v_refs[0].shape[2]), F32)
        for k_ref, v_ref, c0, c1 in chunks:
            s = lax.dot_general(qm[m], k_ref[0, c0:c1, :], dn, preferred_element_type=F32)
            mx_new = jnp.maximum(mx, jnp.max(s, axis=-1, keepdims=True))
            alpha = jnp.exp2(mx - mx_new)
            e = jnp.exp2(s - mx_new)
            den = den * alpha + jnp.sum(e, axis=-1, keepdims=True)
            acc = acc * alpha + jnp.dot(e.astype(BF16), v_ref[0, c0:c1, :], preferred_element_type=F32)
            mx = mx_new
        outs.append(acc / den)
    o = outs[0] - lam * outs[1]
    ms = jnp.mean(o * o, axis=-1, keepdims=True)
    o = o * lax.rsqrt(ms + EPS) * (sg_ref[...] * post_scale)
    o_ref[0] = o.astype(BF16)


def _attention(lam, q_arr, kv_arrs, sub_gain, heads, dqk, post_scale, name):
    b, nq, d3 = q_arr.shape
    d = d3 // 3
    dv = 2 * dqk
    tq = min(nq, ATTN_Q_TILE)
    n_parts = len(kv_arrs)
    kern = functools.partial(_attn_kernel, n_parts=n_parts, dqk=dqk, post_scale=post_scale)
    in_specs = [pl.BlockSpec(memory_space=pltpu.SMEM),
                pl.BlockSpec((1, tq, dv), lambda i, h, t: (i, t, h))]
    in_specs += [pl.BlockSpec((1, a.shape[1], dv), lambda i, h, t: (i, 0, heads + h)) for a in kv_arrs]
    in_specs += [pl.BlockSpec((1, a.shape[1], dv), lambda i, h, t: (i, 0, 2 * heads + h)) for a in kv_arrs]
    in_specs += [pl.BlockSpec((1, dv), lambda i, h, t: (0, 0))]
    return pl.pallas_call(
        kern,
        grid=(b, heads, nq // tq),
        in_specs=in_specs,
        out_specs=pl.BlockSpec((1, tq, dv), lambda i, h, t: (i, t, h)),
        out_shape=jax.ShapeDtypeStruct((b, nq, d), BF16),
        compiler_params=_cparams(("parallel", "parallel", "arbitrary")),
        name=name,
    )(lam, q_arr, *kv_arrs, *kv_arrs, sub_gain)


def _mm_res_kernel(a_ref, w_ref, x_ref, g_ref, o_ref):
    y = jnp.dot(a_ref[0], w_ref[...], preferred_element_type=F32)
    o_ref[0] = x_ref[0] + g_ref[0] * y


def _mm_res(a, w, xs, gate, name):
    b, n, k = a.shape
    d = w.shape[1]
    tm = min(n, 512)
    bi = _bidx(gate.shape[0])
    return pl.pallas_call(
        _mm_res_kernel,
        grid=(b, n // tm),
        in_specs=[pl.BlockSpec((1, tm, k), lambda i, t: (i, t, 0)),
                  pl.BlockSpec((k, d), lambda i, t: (0, 0)),
                  pl.BlockSpec((1, tm, d), lambda i, t: (i, t, 0)),
                  pl.BlockSpec((1, 1, d), lambda i, t: (bi(i), 0, 0))],
        out_specs=pl.BlockSpec((1, tm, d), lambda i, t: (i, t, 0)),
        out_shape=jax.ShapeDtypeStruct((b, n, d), F32),
        compiler_params=_cparams(("parallel", "parallel")),
        name=name,
    )(a, w, xs, gate)


def _gelu_epilogue(acc, j, extra, outs):
    outs[0][0] = jax.nn.gelu(acc).astype(BF16)


def _shift_rows(x, s):
    n = x.shape[0]
    row = lax.broadcasted_iota(jnp.int32, x.shape, 0)
    r = pltpu.roll(x, s % n, axis=0)
    if s >= 0:
        return jnp.where(row >= s, r, 0.0)
    return jnp.where(row < n + s, r, 0.0)


def _conv_epilogue(acc, j, extra, outs):
    cw_ref, cb_ref = extra
    width = cw_ref.shape[0]
    left = width // 2
    y = cb_ref[...] + jnp.zeros_like(acc)
    for k in range(width):
        y = y + cw_ref[k:k + 1, :] * _shift_rows(acc, left - k)
    outs[0][0] = y


def _lru_proj(xs, g, shift, scale, w_gate, w_xr, conv_w, conv_b, name):
    b, n, d = xs.shape
    wd = w_gate.shape[1]
    tn = 256
    gg = _nmm(xs, g, shift, scale, w_gate, tn, _gelu_epilogue, (), [],
              jax.ShapeDtypeStruct((b, n, wd), BF16),
              pl.BlockSpec((1, n, tn), lambda i, j: (i, 0, j)), name + "_gate")
    xc = _nmm(xs, g, shift, scale, w_xr, tn, _conv_epilogue, (conv_w, conv_b),
              [pl.BlockSpec((conv_w.shape[0], tn), lambda i, j: (0, j)),
               pl.BlockSpec((1, tn), lambda i, j: (0, j))],
              jax.ShapeDtypeStruct((b, n, wd), F32),
              pl.BlockSpec((1, n, tn), lambda i, j: (i, 0, j)), name + "_conv")
    return gg, xc


def _lru_scan_kernel(*refs, reverse, reset, final, tt):
    if final:
        xc_ref, wab_ref, bias_ref, lam_ref, h0_ref, prev_ref, gg_ref, o_ref, hfin_ref, a_s, b_s, h_s, st = refs
    else:
        xc_ref, wab_ref, bias_ref, lam_ref, h0_ref, o_ref, hfin_ref, a_s, b_s, h_s, st = refs
    i = pl.program_id(0)
    nb = xc_ref.shape[0]
    ng = wab_ref.shape[0]
    rows = nb * tt
    ts = tt + SCAN_ROW_PAD

    @pl.when(i == 0)
    def _():
        st[...] = h0_ref[...]

    first_t = tt - 1 if reverse else 0
    row = lax.broadcasted_iota(jnp.int32, (rows, LANES), 0)
    for gi in range(ng):
        xg = xc_ref[:, :, gi * LANES:(gi + 1) * LANES].reshape(rows, LANES)
        pre = jnp.dot(xg.astype(BF16), wab_ref[gi], preferred_element_type=F32) + bias_ref[gi]
        r = jax.nn.sigmoid(pre[:, :LANES])
        ig = jax.nn.sigmoid(pre[:, LANES:])
        sp = jax.nn.softplus(-lam_ref[gi])
        log_a = (-LRU_C) * r * sp
        a = jnp.exp(log_a)
        mult = jnp.sqrt(1.0 - a * a)
        if reset:
            mult = jnp.where(jnp.logical_and(row % tt == first_t, i == 0), 1.0, mult)
        bb = mult * ig * xg
        for bi in range(nb):
            a_s[gi, bi * ts:bi * ts + tt, :] = a[bi * tt:(bi + 1) * tt]
            b_s[gi, bi * ts:bi * ts + tt, :] = bb[bi * tt:(bi + 1) * tt]

    def step(s, hs):
        t = (tt - 1 - s) if reverse else s
        out = []
        for gi in range(ng):
            a = a_s[gi, pl.ds(t, nb, stride=ts), :]
            bb = b_s[gi, pl.ds(t, nb, stride=ts), :]
            h = a * hs[gi] + bb
            h_s[gi, pl.ds(t, nb, stride=ts), :] = h
            out.append(h)
        return tuple(out)

    hs = lax.fori_loop(0, tt, step, tuple(st[gi] for gi in range(ng)))
    for gi in range(ng):
        st[gi] = hs[gi]
        hfin_ref[gi] = hs[gi]
        sl = slice(gi * LANES, (gi + 1) * LANES)
        for bi in range(nb):
            hseq = h_s[gi, bi * ts:bi * ts + tt, :]
            if final:
                o_ref[bi, :, sl] = ((prev_ref[bi, :, sl] + hseq) * gg_ref[bi, :, sl].astype(F32)).astype(BF16)
            else:
                o_ref[bi, :, sl] = hseq


def _lru_scan(xc, wab, bias, lam, h0, prev, gg, reverse, reset, name):
    b, n, wd = xc.shape
    ng = wab.shape[0]
    tt = min(n, 32)
    nt = n // tt
    final = prev is not None
    tmap = (lambda i: (0, nt - 1 - i, 0)) if reverse else (lambda i: (0, i, 0))
    full3 = lambda i: (0, 0, 0)
    seq_spec = pl.BlockSpec((b, tt, wd), tmap)
    in_specs = [seq_spec,
                pl.BlockSpec(wab.shape, full3),
                pl.BlockSpec(bias.shape, full3),
                pl.BlockSpec(lam.shape, full3),
                pl.BlockSpec(h0.shape, full3)]
    args = [xc, wab, bias, lam, h0]
    if final:
        in_specs += [seq_spec, seq_spec]
        args += [prev, gg]
    kern = functools.partial(_lru_scan_kernel, reverse=reverse, reset=reset, final=final, tt=tt)
    return pl.pallas_call(
        kern,
        grid=(nt,),
        in_specs=in_specs,
        out_specs=[seq_spec, pl.BlockSpec(h0.shape, full3)],
        out_shape=[jax.ShapeDtypeStruct((b, n, wd), BF16 if final else F32),
                   jax.ShapeDtypeStruct(h0.shape, F32)],
        scratch_shapes=[pltpu.VMEM((ng, b * (tt + SCAN_ROW_PAD), LANES), F32),
                        pltpu.VMEM((ng, b * (tt + SCAN_ROW_PAD), LANES), F32),
                        pltpu.VMEM((ng, b * (tt + SCAN_ROW_PAD), LANES), F32),
                        pltpu.VMEM(h0.shape, F32)],
        compiler_params=_cparams(("arbitrary",)),
        name=name,
    )(*args)


def _window_mean_minus_self(x, window):
    n = x.shape[0]
    back, fwd = x, x
    span = 1
    while span < window // 2:
        back = back + _shift_rows(back, span)
        fwd = fwd + _shift_rows(fwd, -span)
        span *= 2
    p = _shift_rows(back, 1) + fwd
    t = lax.broadcasted_iota(jnp.int32, x.shape, 0)
    lo = jnp.maximum(t - window // 2, 0)
    hi = jnp.minimum(t + window // 2, n)
    cnt = (hi - lo).astype(F32)
    return p / cnt - x


def _pool_epilogue(acc, j, extra, outs):
    wg_ref, ps_ref, xcol_ref, gate_ref = extra
    o_ref = outs[0]
    for gidx, window in enumerate(POOL_WINDOWS):
        @pl.when(j == gidx)
        def _(window=window):
            pooled = _window_mean_minus_self(acc, window).astype(BF16)
            y = jnp.dot(pooled, wg_ref[0], preferred_element_type=F32) * ps_ref[...]
            o_ref[0] = xcol_ref[0] + gate_ref[0] * y


def _pool_mixer(xs, g, shift, scale, w_in, w_grp, pscale, gate, name):
    b, n, d = xs.shape
    ngrp, gd, _ = w_grp.shape
    bi = _bidx(gate.shape[0])
    extra = (w_grp, pscale, xs, gate)
    extra_specs = [pl.BlockSpec((1, gd, gd), lambda i, j: (j, 0, 0)),
                   pl.BlockSpec((1, gd), lambda i, j: (0, j)),
                   pl.BlockSpec((1, n, gd), lambda i, j: (i, 0, j)),
                   pl.BlockSpec((1, 1, gd), lambda i, j: (bi(i), 0, j))]
    return _nmm(xs, g, shift, scale, w_in, gd, _pool_epilogue, extra, extra_specs,
                jax.ShapeDtypeStruct((b, n, d), F32),
                pl.BlockSpec((1, n, gd), lambda i, j: (i, 0, j)), name)


def _route_kernel(x_ref, g_ref, sh_ref, sc_ref, rhi_ref, rlo_ref, u_ref, meta_ref, cnt_ref, *, n_groups):
    u = _norm_mod(x_ref[0], g_ref[...], sh_ref[0], sc_ref[0])
    tm = u.shape[0]
    n_chunks = u.shape[1] // LANES
    for jc in range(n_chunks):
        u_ref[pl.ds(jc, tm, stride=n_chunks), :] = u[:, jc * LANES:(jc + 1) * LANES]
    u_hi = u.astype(BF16)
    u_lo = (u - u_hi.astype(F32)).astype(BF16)
    lg = (jnp.dot(u_hi, rhi_ref[...], preferred_element_type=F32)
          + jnp.dot(u_lo, rhi_ref[...], preferred_element_type=F32)
          + jnp.dot(u_hi, rlo_ref[...], preferred_element_type=F32))
    lane = lax.broadcasted_iota(jnp.int32, lg.shape, 1)
    neg = jnp.float32(-jnp.inf)
    big = jnp.int32(1 << 20)
    is_g = lane < n_groups
    gmax = jnp.max(jnp.where(is_g, lg, neg), axis=-1, keepdims=True)
    g_sel = jnp.min(jnp.where(jnp.logical_and(is_g, lg == gmax), lane, big), axis=-1, keepdims=True)
    p_sel = 1.0 / jnp.sum(jnp.where(is_g, jnp.exp(lg - gmax), 0.0), axis=-1, keepdims=True)
    e_idx = lane - n_groups
    in_grp = jnp.logical_and(e_idx >= g_sel * EXPERTS_PER_GROUP, e_idx < (g_sel + 1) * EXPERTS_PER_GROUP)
    m1 = jnp.max(jnp.where(in_grp, lg, neg), axis=-1, keepdims=True)
    i1 = jnp.min(jnp.where(jnp.logical_and(in_grp, lg == m1), lane, big), axis=-1, keepdims=True)
    rest = jnp.logical_and(in_grp, lane != i1)
    m2 = jnp.max(jnp.where(rest, lg, neg), axis=-1, keepdims=True)
    i2 = jnp.min(jnp.where(jnp.logical_and(rest, lg == m2), lane, big), axis=-1, keepdims=True)
    e2 = jnp.exp(m2 - m1)
    w1 = p_sel / (1.0 + e2)
    w2 = w1 * e2
    off = n_groups + g_sel * EXPERTS_PER_GROUP
    la = jnp.minimum(i1, i2) - off
    lb = jnp.maximum(i1, i2) - off
    pair = ((la * (2 * EXPERTS_PER_GROUP - 1 - la)) >> 1) + lb - la - 1
    cls = g_sel * PAIRS_PER_GROUP + pair
    c_lo = jnp.where(i1 < i2, w1, w2)
    c_hi = jnp.where(i1 < i2, w2, w1)
    onehot = lane == cls
    row = lax.broadcasted_iota(jnp.int32, (tm, tm), 0)
    col = lax.broadcasted_iota(jnp.int32, (tm, tm), 1)
    tri = jnp.where(row > col, 1.0, 0.0).astype(BF16)
    before = jnp.dot(tri, jnp.where(onehot, 1.0, 0.0).astype(BF16), preferred_element_type=F32)
    rank = jnp.sum(jnp.where(onehot, before, 0.0), axis=-1, keepdims=True)
    meta_ref[0] = jnp.where(lane == 0, cls.astype(F32),
                            jnp.where(lane == 1, rank,
                                      jnp.where(lane == 2, c_lo, jnp.where(lane == 3, c_hi, 0.0))))
    cnt = jnp.sum(jnp.where(onehot, 1.0, 0.0), axis=0, keepdims=True)
    cnt_ref[0] = jnp.broadcast_to(cnt, cnt_ref.shape[1:])


def _route(xs, g, shift, scale, r_hi, r_lo, n_groups, name):
    b, n, d = xs.shape
    tm = min(n, ROUTE_BLOCK)
    n_chunks = d // LANES
    nt = n // tm
    bi = _bidx(shift.shape[0])
    kern = functools.partial(_route_kernel, n_groups=n_groups)
    return pl.pallas_call(
        kern,
        grid=(b, nt),
        in_specs=[pl.BlockSpec((1, tm, d), lambda i, t: (i, t, 0)),
                  pl.BlockSpec((1, d), lambda i, t: (0, 0)),
                  pl.BlockSpec((1, 1, d), lambda i, t: (bi(i), 0, 0)),
                  pl.BlockSpec((1, 1, d), lambda i, t: (bi(i), 0, 0)),
                  pl.BlockSpec((d, LANES), lambda i, t: (0, 0)),
                  pl.BlockSpec((d, LANES), lambda i, t: (0, 0))],
        out_specs=[pl.BlockSpec((tm * n_chunks, LANES), lambda i, t: (i * nt + t, 0)),
                   pl.BlockSpec((1, tm, LANES), lambda i, t: (i, t, 0)),
                   pl.BlockSpec((1, 8, LANES), lambda i, t: (i * nt + t, 0, 0))],
        out_shape=[jax.ShapeDtypeStruct((b * n * n_chunks, LANES), F32),
                   jax.ShapeDtypeStruct((b, n, LANES), F32),
                   jax.ShapeDtypeStruct((b * nt, 8, LANES), F32)],
        compiler_params=_cparams(("parallel", "parallel")),
        name=name,
    )(xs, g, shift, scale, r_hi, r_lo)


def _burst_dma(make, n, wait_all):
    def start(i, c):
        for pr in range(2):
            make(2 * i + pr, pr).start(priority=pr)
        return c

    lax.fori_loop(0, n // 2, start, 0)
    for pr in range(2):
        wait_all(pr)


def _dispatch_kernel(base_ref, cls_ref, rank_ref, u_ref, buf_in_ref, buf_ref, sems, *, blk0, n_cls):
    del buf_in_ref
    tb = cls_ref.shape[-1]
    rows = buf_ref.shape[1]
    boff = (blk0 + pl.program_id(0)) * n_cls

    def make(k, slot):
        pos = base_ref[boff + cls_ref[0, 0, k]] + rank_ref[0, 0, k]
        src = u_ref.at[pl.ds(pl.multiple_of(k * rows, rows), rows)]
        return pltpu.make_async_copy(src, buf_ref.at[pos], sems.at[slot])

    def wait_all(slot):
        half = buf_ref.at[pl.ds(0, tb // 2)]
        pltpu.make_async_copy(half, half, sems.at[slot]).wait()

    _burst_dma(make, tb, wait_all)


def _dispatch(base, cls, rank, u_flat, buf, blk0, n_cls, name):
    nb, _, tb = cls.shape
    rows = buf.shape[1]
    smem = lambda: pl.BlockSpec((1, 1, tb), lambda i, bs: (i, 0, 0), memory_space=pltpu.SMEM)
    grid_spec = pltpu.PrefetchScalarGridSpec(
        num_scalar_prefetch=1,
        grid=(nb,),
        in_specs=[smem(), smem(),
                  pl.BlockSpec((tb * rows, LANES), lambda i, bs: (i, 0)),
                  pl.BlockSpec(memory_space=pl.ANY)],
        out_specs=pl.BlockSpec(memory_space=pl.ANY),
        scratch_shapes=[pltpu.SemaphoreType.DMA((DMA_QUEUES,))],
    )
    return pl.pallas_call(
        functools.partial(_dispatch_kernel, blk0=blk0, n_cls=n_cls),
        grid_spec=grid_spec,
        out_shape=jax.ShapeDtypeStruct(buf.shape, buf.dtype),
        input_output_aliases={4: 0},
        compiler_params=pltpu.CompilerParams(dimension_semantics=("arbitrary",), has_side_effects=True),
        name=name,
    )(base, cls, rank, u_flat, buf)


def _experts_kernel(ea_ref, eb_ref, nused_ref, x_ref, w1a_ref, w3a_ref, w2a_ref, w1b_ref, w3b_ref, w2b_ref,
                    y_ref):
    del ea_ref, eb_ref
    k = pl.program_id(0)
    n_chunks = w1a_ref.shape[1] // LANES
    mt = x_ref.shape[0] // n_chunks

    @pl.when(k < nused_ref[0])
    def _():
        x = jnp.concatenate([x_ref[pl.ds(jc, mt, stride=n_chunks), :] for jc in range(n_chunks)],
                            axis=-1).astype(BF16)
        for half, (w1_ref, w3_ref, w2_ref) in enumerate(((w1a_ref, w3a_ref, w2a_ref),
                                                         (w1b_ref, w3b_ref, w2b_ref))):
            h1 = jnp.dot(x, w1_ref[0], preferred_element_type=F32)
            h3 = jnp.dot(x, w3_ref[0], preferred_element_type=F32)
            he = (h1 * jax.nn.sigmoid(h1) * h3).astype(BF16)
            y = jnp.dot(he, w2_ref[0], preferred_element_type=F32)
            for jc in range(n_chunks):
                y_ref[pl.ds(half * n_chunks + jc, mt, stride=2 * n_chunks), :] = y[:, jc * LANES:(jc + 1) * LANES]

    @pl.when(k >= nused_ref[0])
    def _():
        y_ref[...] = jnp.zeros_like(y_ref)


def _experts(tile_ea, tile_eb, n_used, xbuf, w1, w3, w2, mt, name):
    ne, d, f = w1.shape
    n_chunks = d // LANES
    n_tiles = xbuf.shape[0] // (mt * n_chunks)
    wa = lambda k, ea, eb, nu: (ea[k], 0, 0)
    wb = lambda k, ea, eb, nu: (eb[k], 0, 0)
    grid_spec = pltpu.PrefetchScalarGridSpec(
        num_scalar_prefetch=3,
        grid=(n_tiles,),
        in_specs=[pl.BlockSpec((mt * n_chunks, LANES), lambda k, ea, eb, nu: (k, 0)),
                  pl.BlockSpec((1, d, f), wa), pl.BlockSpec((1, d, f), wa), pl.BlockSpec((1, f, d), wa),
                  pl.BlockSpec((1, d, f), wb), pl.BlockSpec((1, d, f), wb), pl.BlockSpec((1, f, d), wb)],
        out_specs=pl.BlockSpec((mt * 2 * n_chunks, LANES), lambda k, ea, eb, nu: (k, 0)),
    )
    return pl.pallas_call(
        _experts_kernel,
        grid_spec=grid_spec,
        out_shape=jax.ShapeDtypeStruct((n_tiles * mt * 2 * n_chunks, LANES), F32),
        compiler_params=_cparams(("arbitrary",)),
        name=name,
    )(tile_ea, tile_eb, n_used, xbuf, w1, w3, w2, w1, w3, w2)


def _combine_kernel(base_ref, cls_ref, rank_ref, y_hbm, meta_ref, x_ref, g_ref, o_ref, buf, sems, *, blk0, n_cls):
    tb = cls_ref.shape[-1]
    rows = y_hbm.shape[1]
    n_chunks = rows // 2
    boff = (blk0 + pl.program_id(0) * pl.num_programs(1) + pl.program_id(1)) * n_cls

    def make(k, slot):
        pos = base_ref[boff + cls_ref[0, 0, k]] + rank_ref[0, 0, k]
        dst = buf.at[pl.ds(pl.multiple_of(k * rows, rows), rows)]
        return pltpu.make_async_copy(y_hbm.at[pos], dst, sems.at[slot])

    def wait_all(slot):
        half = buf.at[pl.ds(0, (tb // 2) * rows)]
        pltpu.make_async_copy(half, half, sems.at[slot]).wait()

    _burst_dma(make, tb, wait_all)
    meta = meta_ref[0]
    c_lo = meta[:, 2:3]
    c_hi = meta[:, 3:4]
    for jc in range(n_chunks):
        sl = slice(jc * LANES, (jc + 1) * LANES)
        y_lo = buf[pl.ds(jc, tb, stride=rows), :]
        y_hi = buf[pl.ds(n_chunks + jc, tb, stride=rows), :]
        o_ref[0, :, sl] = x_ref[0, :, sl] + g_ref[0, :, sl] * (c_lo * y_lo + c_hi * y_hi)


def _combine(base, cls, rank, y3, meta, xs, gate, blk0, n_cls, name):
    b, n, d = xs.shape
    tb = cls.shape[-1]
    nt = n // tb
    rows = y3.shape[1]
    bi = _bidx(gate.shape[0])
    smem = lambda: pl.BlockSpec((1, 1, tb), lambda i, t, bs: (i * nt + t, 0, 0), memory_space=pltpu.SMEM)
    grid_spec = pltpu.PrefetchScalarGridSpec(
        num_scalar_prefetch=1,
        grid=(b, nt),
        in_specs=[smem(), smem(),
                  pl.BlockSpec(memory_space=pl.ANY),
                  pl.BlockSpec((1, tb, LANES), lambda i, t, bs: (i, t, 0)),
                  pl.BlockSpec((1, tb, d), lambda i, t, bs: (i, t, 0)),
                  pl.BlockSpec((1, 1, d), lambda i, t, bs: (bi(i), 0, 0))],
        out_specs=pl.BlockSpec((1, tb, d), lambda i, t, bs: (i, t, 0)),
        scratch_shapes=[pltpu.VMEM((tb * rows, LANES), F32),
                        pltpu.SemaphoreType.DMA((DMA_QUEUES,))],
    )
    return pl.pallas_call(
        functools.partial(_combine_kernel, blk0=blk0, n_cls=n_cls),
        grid_spec=grid_spec,
        out_shape=jax.ShapeDtypeStruct((b, n, d), F32),
        compiler_params=_cparams(("arbitrary", "arbitrary")),
        name=name,
    )(base, cls, rank, y3, meta, xs, gate)


def _hier_moe(streams, g, r_hi, r_lo, w1, w3, w2, n_groups, name):
    d = w1.shape[1]
    n_chunks = d // LANES
    n_cls = n_groups * PAIRS_PER_GROUP
    routed = [_route(xs, g, sh, sc, r_hi, r_lo, n_groups, f"{name}_route{si}")
              for si, (xs, sh, sc, _) in enumerate(streams)]
    cnt = jnp.concatenate([r[2][:, 0, :n_cls] for r in routed], axis=0).astype(jnp.int32)
    total = jnp.sum(cnt, axis=0)
    padded = ((total + MOE_TILE - 1) // MOE_TILE) * MOE_TILE
    cls_end = jnp.cumsum(padded)
    base = (cls_end - padded)[None, :] + jnp.cumsum(cnt, axis=0) - cnt
    n_tok = sum(xs.shape[0] * xs.shape[1] for xs, _, _, _ in streams)
    n_tiles = n_tok // MOE_TILE + n_cls
    tile_start = jnp.arange(n_tiles, dtype=jnp.int32) * MOE_TILE
    n_used = (cls_end[-1] // MOE_TILE).astype(jnp.int32)
    last_start = jnp.maximum(cls_end[-1] - MOE_TILE, 0)
    tile_cls = jnp.sum((jnp.minimum(tile_start, last_start)[:, None] >= cls_end[None, :]).astype(jnp.int32), axis=1)
    tile_cls = jnp.minimum(tile_cls, n_cls - 1)
    pairs = [(a, b) for a in range(EXPERTS_PER_GROUP) for b in range(a + 1, EXPERTS_PER_GROUP)]
    lo_tab = jnp.array([gi * EXPERTS_PER_GROUP + a for gi in range(n_groups) for a, _ in pairs], jnp.int32)
    hi_tab = jnp.array([gi * EXPERTS_PER_GROUP + b for gi in range(n_groups) for _, b in pairs], jnp.int32)
    tile_ea = lo_tab[tile_cls]
    tile_eb = hi_tab[tile_cls]

    base_flat = base.reshape(-1).astype(jnp.int32)
    buf = jnp.zeros((n_tiles * MOE_TILE, n_chunks, LANES), F32)
    idx, blk0 = [], 0
    for si, ((xs, _, _, _), (u_flat, meta, c)) in enumerate(zip(streams, routed)):
        nb = c.shape[0]
        tm = xs.shape[1] * xs.shape[0] // nb
        cls_t = meta[..., 0].astype(jnp.int32).reshape(nb, 1, tm)
        rank_t = meta[..., 1].astype(jnp.int32).reshape(nb, 1, tm)
        idx.append((cls_t, rank_t, blk0))
        buf = _dispatch(base_flat, cls_t, rank_t, u_flat, buf, blk0, n_cls, f"{name}_dispatch{si}")
        blk0 += nb
    y = _experts(tile_ea, tile_eb, n_used.reshape(1), buf.reshape(-1, LANES), w1, w3, w2, MOE_TILE,
                 f"{name}_experts")
    y3 = y.reshape(-1, 2 * n_chunks, LANES)
    outs = []
    for si, ((xs, _, _, gate), (_, meta, _), (cls_t, rank_t, b0)) in enumerate(zip(streams, routed, idx)):
        outs.append(_combine(base_flat, cls_t, rank_t, y3, meta, xs, gate, b0, n_cls, f"{name}_combine{si}"))
    return outs


def _rope_tables(n, dqk):
    rows = n // GRID_W
    row = jnp.repeat(jnp.arange(rows), GRID_W).astype(F32)
    col = jnp.tile(jnp.arange(GRID_W), rows).astype(F32)
    n_freq = dqk // 4
    inv = ROPE_BASE ** (-jnp.arange(n_freq, dtype=F32) / n_freq)
    ang = jnp.concatenate([row[:, None] * inv, col[:, None] * inv], axis=-1)
    cos, sin = jnp.cos(ang), jnp.sin(ang)
    reps = LANES // dqk
    cos_l = jnp.tile(jnp.concatenate([cos, cos], axis=-1), (1, reps))
    sin_l = jnp.tile(jnp.concatenate([-sin, sin], axis=-1), (1, reps))
    return cos_l, sin_l


def kernel(x, c, ctx, c_ctx, w_ada, b_ada, norm_g, attn_w_in, attn_q_gain, attn_k_gain, attn_lam, attn_sub_gain, attn_w_out, lru_w_in, lru_conv_w, lru_conv_b, lru_w_a, lru_b_a, lru_w_x, lru_b_x, lru_lam, lru_w_out, pool_w_in, pool_w_grp, pool_scale, moe_router_g, moe_router_e, moe_w1, moe_w3, moe_w2):
    bsz, seq, d = x.shape
    n_ctx = ctx.shape[1]
    depth = w_ada.shape[0]
    dqk = attn_q_gain.shape[-1]
    heads = d // (2 * dqk)
    n_groups = moe_router_g.shape[-1]
    n_experts = moe_router_e.shape[-1]
    lru_w = lru_w_out.shape[1]
    lru_g = lru_w_a.shape[2]

    pad = (-(bsz + 1)) % 8
    cc = jnp.concatenate([c, c_ctx[None, :], jnp.zeros((pad, d), F32)], axis=0)
    mods = _ada(cc, w_ada, b_ada)

    cos_l, sin_l = _rope_tables(seq, dqk)
    cos_c = jnp.ones((n_ctx, LANES), F32)
    sin_c = jnp.zeros((n_ctx, LANES), F32)
    lane = jnp.arange(LANES)
    gmat = (lane[:, None] // dqk == lane[None, :] // dqk).astype(BF16)

    xs, cs = x, ctx
    for i in range(depth):
        need_ctx = i < depth - 1
        mx = [mods[i, :bsz, k * d:(k + 1) * d][:, None, :] for k in range(6)]
        mc = [mods[i, bsz:bsz + 1, k * d:(k + 1) * d][:, None, :] for k in range(6)]
        g1 = norm_g[i, 0][None, :]
        g2 = norm_g[i, 1][None, :]
        kind, j = i % 3, i // 3
        nm = f"l{i}"
        if kind == 0:
            lam_init = 0.8 - 0.6 * math.exp(-0.3 * i)
            lq = attn_lam[j].astype(F32)
            lam = (jnp.exp(jnp.sum(lq[0] * lq[1])) - jnp.exp(jnp.sum(lq[2] * lq[3])) + lam_init).reshape(1)
            w_in = attn_w_in[j].astype(BF16)
            reps = LANES // dqk
            gains = jnp.stack([jnp.tile(attn_q_gain[j], reps) * (dqk ** -0.5 * math.log2(math.e)),
                               jnp.tile(attn_k_gain[j], reps)])[:, None, :]
            sub_gain = attn_sub_gain[j][None, :]
            qkv_x = _qkv_proj(xs, g1, mx[0], mx[1], w_in, gains, cos_l, sin_l, gmat, True, dqk, nm + "_qkv_x")
            qkv_c = _qkv_proj(cs, g1, mc[0], mc[1], w_in, gains, cos_c, sin_c, gmat, False, dqk, nm + "_qkv_c")
            w_out = attn_w_out[j].astype(BF16)
            ox = _attention(lam, qkv_x, [qkv_c, qkv_x], sub_gain, heads, dqk, 1.0 - lam_init, nm + "_attn_x")
            xs = _mm_res(ox, w_out, xs, mx[2], nm + "_out_x")
            if need_ctx:
                oc = _attention(lam, qkv_c, [qkv_c], sub_gain, heads, dqk, 1.0 - lam_init, nm + "_attn_c")
                cs = _mm_res(oc, w_out, cs, mc[2], nm + "_out_c")
        elif kind == 1:
            w_in = lru_w_in[j].astype(BF16)
            w_gate, w_xr = w_in[:, :lru_w], w_in[:, lru_w:]
            conv_b = lru_conv_b[j][None, :]
            ggx, xcx = _lru_proj(xs, g1, mx[0], mx[1], w_gate, w_xr, lru_conv_w[j], conv_b, nm + "_proj_x")
            ggc, xcc = _lru_proj(cs, g1, mc[0], mc[1], w_gate, w_xr, lru_conv_w[j], conv_b, nm + "_proj_c")
            h0 = jnp.zeros((lru_g, bsz, LANES), F32)
            hx_prev, hc_prev = None, None
            for dr, reverse in enumerate((False, True)):
                wab = jnp.concatenate([lru_w_a[j, dr], lru_w_x[j, dr]], axis=-1).astype(BF16)
                bias = jnp.concatenate([lru_b_a[j, dr].reshape(lru_g, 1, LANES),
                                        lru_b_x[j, dr].reshape(lru_g, 1, LANES)], axis=-1)
                lam_d = lru_lam[j, dr].reshape(lru_g, 1, LANES)
                last = dr == 1
                hc_seq, hc_fin = _lru_scan(xcc, wab, bias, lam_d, h0, hc_prev if last else None,
                                           ggc if last else None, reverse, True, f"{nm}_scan_c{dr}")
                hx_seq, _ = _lru_scan(xcx, wab, bias, lam_d, hc_fin, hx_prev if last else None,
                                      ggx if last else None, reverse, False, f"{nm}_scan_x{dr}")
                hx_prev, hc_prev = hx_seq, hc_seq
            w_out = lru_w_out[j].astype(BF16)
            xs = _mm_res(hx_prev, w_out, xs, mx[2], nm + "_out_x")
            if need_ctx:
                cs = _mm_res(hc_prev, w_out, cs, mc[2], nm + "_out_c")
        else:
            w_in = pool_w_in[j].astype(BF16)
            w_grp = pool_w_grp[j].astype(BF16)
            pscale = pool_scale[j][None, :]
            xs_new = _pool_mixer(xs, g1, mx[0], mx[1], w_in, w_grp, pscale, mx[2], nm + "_pool_x")
            if need_ctx:
                cs = _pool_mixer(cs, g1, mc[0], mc[1], w_in, w_grp, pscale, mc[2], nm + "_pool_c")
            xs = xs_new

        r_all = jnp.concatenate([moe_router_g[i], moe_router_e[i],
                                 jnp.zeros((d, LANES - n_groups - n_experts), F32)], axis=-1)
        r_hi = r_all.astype(BF16)
        r_lo = (r_all - r_hi.astype(F32)).astype(BF16)
        w1 = moe_w1[i].astype(BF16)
        w3 = moe_w3[i].astype(BF16)
        w2 = moe_w2[i].astype(BF16)
        streams = [(xs, mx[3], mx[4], mx[5])]
        if need_ctx:
            streams.append((cs, mc[3], mc[4], mc[5]))
        outs = _hier_moe(streams, g2, r_hi, r_lo, w1, w3, w2, n_groups, nm + "_moe")
        xs = outs[0]
        if need_ctx:
            cs = outs[1]
    return xs
```
